```python
import jax, jax.numpy as jnp
from jax import lax
import numpy as np

D_MODEL = 1024
BATCH = 1
SEQ = 16384
DEPTH = 2
DEC_BATCH = 32
DEC_SEQ = 8
PAST_LEN = 16384
PAGE_SIZE = 128

D_CONV = 3 * D_MODEL // 8
CONV_WIDTH = 31
CONV_STATE = CONV_WIDTH - 1
HEAD_DIM = 64
DIL_GROUPS = ((128, 1), (512, 4), (2048, 16))
HEADS_PER_GROUP = 4
N_HEADS = HEADS_PER_GROUP * len(DIL_GROUPS)
D_ATTN = N_HEADS * HEAD_DIM
Q_BLOCK = 128
N_BUCKETS = 32
MAX_DISTANCE = 2048
D_SGU = 3 * D_MODEL // 8
SGU_GROUPS = 4
CHUNK = 128
D_POOL = 3 * D_MODEL // 8
POOL_WINDOWS = (2, 4, 8, 16)
POOL_STATE = max(POOL_WINDOWS) - 1
D_FF = ((8 * D_MODEL // 3 + 255) // 256) * 256
D_PLE = 256
N_BRANCH = 4
SPLITS = (2 * D_CONV,
          2 * D_CONV + D_ATTN,
          2 * D_CONV + 2 * D_ATTN,
          2 * D_CONV + 3 * D_ATTN,
          2 * D_CONV + 3 * D_ATTN + 2 * D_SGU,
          2 * D_CONV + 3 * D_ATTN + 2 * D_SGU + D_POOL)
D_IN = SPLITS[-1] + N_BRANCH * D_MODEL
EPS = 1e-6
NEG_INF = -1e30

kernel_name = "hybrid_gated_conv_dilattn_sgu_pool_decoder_step"


def _rmsnorm(x, g):
    xf = x.astype(jnp.float32)
    y = xf * lax.rsqrt(jnp.mean(xf * xf, axis=-1, keepdims=True) + EPS)
    return (y * g.astype(jnp.float32)).astype(x.dtype)


def _layernorm(x, g, b):
    xf = x.astype(jnp.float32)
    mu = jnp.mean(xf, axis=-1, keepdims=True)
    xc = xf - mu
    y = xc * lax.rsqrt(jnp.mean(xc * xc, axis=-1, keepdims=True) + 1e-5)
    return (y * g.astype(jnp.float32) + b.astype(jnp.float32)).astype(x.dtype)


def _t5_bucket(dist):
    max_exact = N_BUCKETS // 2
    d = np.asarray(dist)
    large = max_exact + (np.log(np.maximum(d, 1) / max_exact)
                         / np.log(MAX_DISTANCE / max_exact)
                         * (N_BUCKETS - max_exact)).astype(np.int64)
    large = np.minimum(large, N_BUCKETS - 1)
    return np.where(d < max_exact, d, large).astype(np.int32)


def _dilated_attention(q, k, v, qidx, offsets, bias):
    B, T, H, hd = q.shape
    qb = Q_BLOCK if T % Q_BLOCK == 0 else T
    nb = T // qb
    offs = jnp.asarray(offsets, jnp.int32)
    bias_t = bias.T.astype(jnp.float32)
    scale = HEAD_DIM ** -0.5

    def block(args):
        qblk, qi = args
        idx = qi[:, None] - offs[None, :]
        valid = idx >= 0
        idx = jnp.maximum(idx, 0)
        kg = jnp.take(k, idx, axis=1)
        vg = jnp.take(v, idx, axis=1)
        s = jnp.einsum("bqhd,bqkhd->bqhk", qblk, kg).astype(jnp.float32) * scale + bias_t
        s = jnp.where(valid[None, :, None, :], s, NEG_INF)
        m = jnp.max(s, axis=-1, keepdims=True)
        p = jnp.exp(s - m)
        den = jnp.sum(p, axis=-1, keepdims=True)
        o = jnp.einsum("bqhk,bqkhd->bqhd", p, vg.astype(jnp.float32)) / den
        return o, (m + jnp.log(den))[..., 0]

    qs = q.reshape(B, nb, qb, H, hd).swapaxes(0, 1)
    qis = qidx.reshape(nb, qb)
    o, lse = lax.map(block, (qs, qis))
    return o.swapaxes(0, 1).reshape(B, T, H, hd), lse.swapaxes(0, 1).reshape(B, T, H)


def _layer(x, pe, prev_conv, prev_pool, prev_kv, pos0, lp, rel_bias):
    B, T, _ = x.shape
    dt = x.dtype
    h = _rmsnorm(x, lp["norm_mix"])
    z = h @ lp["w_in"]
    za, zq, zk, zv, zc, zd, zg = jnp.split(z, SPLITS, axis=-1)

    a, b = jnp.split(za, 2, axis=-1)
    glu = a * jax.nn.sigmoid(b)
    ext = jnp.concatenate([prev_conv.astype(dt), glu], axis=1)
    cv = lax.conv_general_dilated(ext, lp["conv_w"][:, None, :].astype(dt), (1,), "VALID",
                                  dimension_numbers=("NWC", "WIO", "NWC"),
                                  feature_group_count=D_CONV) + lp["conv_b"]
    cv = jax.nn.silu(_layernorm(cv, lp["conv_ln_g"], lp["conv_ln_b"]))
    br_a = cv @ lp["w_a_out"]
    new_conv = ext[:, ext.shape[1] - CONV_STATE:]

    q = _rmsnorm(zq.reshape(B, T, N_HEADS, HEAD_DIM), lp["q_norm"])
    k = _rmsnorm(zk.reshape(B, T, N_HEADS, HEAD_DIM), lp["k_norm"])
    v = zv.reshape(B, T, N_HEADS, HEAD_DIM)
    outs, lses, new_kv = [], [], []
    for gi, (win, dil) in enumerate(DIL_GROUPS):
        hs = slice(gi * HEADS_PER_GROUP, (gi + 1) * HEADS_PER_GROUP)
        kv_new = jnp.stack([k[:, :, hs], v[:, :, hs]], axis=2)
        L = prev_kv[gi].shape[1]
        kv_ext = jnp.concatenate([prev_kv[gi].astype(dt), kv_new], axis=1)
        offsets = dil * np.arange(win // dil + 1)
        bias = rel_bias[_t5_bucket(offsets)][:, hs]
        qidx = L + jnp.arange(T, dtype=jnp.int32)
        o, lse = _dilated_attention(q[:, :, hs], kv_ext[:, :, 0], kv_ext[:, :, 1], qidx, offsets, bias)
        outs.append(o)
        lses.append(lse)
        keep = L if L > 0 else min(win, T)
        new_kv.append(kv_ext[:, kv_ext.shape[1] - keep:])
    alpha = jax.nn.softmax(jnp.stack(lses, axis=0), axis=0)
    o = jnp.sum(alpha[..., None] * jnp.stack(outs, axis=0), axis=0)
    br_b = o.reshape(B, T, HEADS_PER_GROUP * HEAD_DIM).astype(dt) @ lp["w_b_out"]

    u, vv = jnp.split(zc, 2, axis=-1)
    vv = _layernorm(vv, lp["sgu_ln_g"], lp["sgu_ln_b"])
    cl = min(T, CHUNK)
    nc = T // cl
    tril = np.tril(np.ones((cl, cl), dtype=bool))
    ws = jnp.where(tril[None], lp["sgu_w"][:, :cl, :cl], 0)
    vr = vv.reshape(B, nc, cl, SGU_GROUPS, D_SGU // SGU_GROUPS)
    sv = jnp.einsum("gij,bcjgd->bcigd", ws, vr) + lp["sgu_b"][:, :cl].T[None, None, :, :, None]
    br_c = (u * sv.reshape(B, T, D_SGU)) @ lp["w_c_out"]

    P = POOL_STATE
    ext_d = jnp.concatenate([prev_pool.astype(dt), zd], axis=1)
    xf = ext_d.astype(jnp.float32)
    S = jnp.concatenate([jnp.zeros((B, 1, D_POOL), jnp.float32), jnp.cumsum(xf, axis=1)], axis=1)
    pos = pos0 + jnp.arange(T)
    cur = xf[:, P:]
    gw = D_POOL // len(POOL_WINDOWS)
    parts = []
    for j, w in enumerate(POOL_WINDOWS):
        cs = slice(j * gw, (j + 1) * gw)
        win_sum = S[:, P + 1:P + 1 + T, cs] - S[:, P + 1 - w:P + 1 - w + T, cs]
        cnt = jnp.minimum(w, pos + 1).astype(jnp.float32)
        parts.append(win_sum / cnt[None, :, None] - cur[:, :, cs])
    pooled = jnp.stack(parts, axis=2).astype(dt)
    mixed = jnp.einsum("btgc,gcd->btgd", pooled, lp["pool_w"]).reshape(B, T, D_POOL) * lp["pool_scale"]
    br_d = mixed @ lp["w_d_out"]
    new_pool = ext_d[:, ext_d.shape[1] - P:]

    gates = jax.nn.sigmoid(zg.reshape(B, T, N_BRANCH, D_MODEL))
    merged = (gates[:, :, 0] * br_a + gates[:, :, 1] * br_b
              + gates[:, :, 2] * br_c + gates[:, :, 3] * br_d)
    x = x + merged @ lp["w_o"]
    h2 = _rmsnorm(x, lp["norm_ffn"])
    x = x + (jax.nn.silu(h2 @ lp["w_gate"]) * (h2 @ lp["w_up"])) @ lp["w_down"]
    x = x + (pe @ lp["w_ple"]) * jax.nn.sigmoid(_rmsnorm(x, lp["ple_norm"]) @ lp["w_ple_gate"])
    return x, new_kv, new_conv, new_pool, vv


def setup_inputs(seed: int = 0) -> dict:
    key = jax.random.key(seed)
    ks = iter(jax.random.split(key, 64))

    def nrm(shape, scale=1.0):
        return scale * jax.random.normal(next(ks), shape, jnp.float32)

    def gain(shape):
        return 1.0 + 0.05 * nrm(shape)

    lens = [min(w, PAST_LEN) for w, _ in DIL_GROUPS]
    return {
        "x_prompt": nrm((BATCH, SEQ, D_MODEL)),
        "x_sample": nrm((DEC_BATCH, DEC_SEQ, D_MODEL)),
        "cache_kv_w128": nrm((DEPTH, DEC_BATCH, lens[0], 2, HEADS_PER_GROUP, HEAD_DIM)),
        "cache_kv_w512": nrm((DEPTH, DEC_BATCH, lens[1], 2, HEADS_PER_GROUP, HEAD_DIM)),
        "cache_kv_w2048": nrm((DEPTH, DEC_BATCH, lens[2], 2, HEADS_PER_GROUP, HEAD_DIM)),
        "state_conv": nrm((DEPTH, DEC_BATCH, CONV_STATE, D_CONV), 0.5),
        "state_pool": nrm((DEPTH, DEC_BATCH, POOL_STATE, D_POOL)),
        "p_prompt": nrm((DEPTH, BATCH, SEQ, D_PLE)),
        "p_sample": nrm((DEPTH, DEC_BATCH, DEC_SEQ, D_PLE)),
        "rel_bias": nrm((N_BUCKETS, N_HEADS), 0.2),
        "norm_mix": gain((DEPTH, D_MODEL)),
        "w_in": nrm((DEPTH, D_MODEL, D_IN), D_MODEL ** -0.5),
        "conv_w": nrm((DEPTH, CONV_WIDTH, D_CONV), CONV_WIDTH ** -0.5),
        "conv_b": nrm((DEPTH, D_CONV), 0.02),
        "conv_ln_g": gain((DEPTH, D_CONV)),
        "conv_ln_b": nrm((DEPTH, D_CONV), 0.02),
        "w_a_out": nrm((DEPTH, D_CONV, D_MODEL), D_CONV ** -0.5),
        "q_norm": gain((DEPTH, HEAD_DIM)),
        "k_norm": gain((DEPTH, HEAD_DIM)),
        "w_b_out": nrm((DEPTH, HEADS_PER_GROUP * HEAD_DIM, D_MODEL), (HEADS_PER_GROUP * HEAD_DIM) ** -0.5),
        "sgu_ln_g": gain((DEPTH, D_SGU)),
        "sgu_ln_b": nrm((DEPTH, D_SGU), 0.02),
        "sgu_w": nrm((DEPTH, SGU_GROUPS, CHUNK, CHUNK), CHUNK ** -0.5),
        "sgu_b": 1.0 + 0.1 * nrm((DEPTH, SGU_GROUPS, CHUNK)),
        "w_c_out": nrm((DEPTH, D_SGU, D_MODEL), D_SGU ** -0.5),
        "pool_w": nrm((DEPTH, len(POOL_WINDOWS), D_POOL // len(POOL_WINDOWS), D_POOL // len(POOL_WINDOWS)),
                      (D_POOL // len(POOL_WINDOWS)) ** -0.5),
        "pool_scale": 1.0 + 0.1 * nrm((DEPTH, D_POOL)),
        "w_d_out": nrm((DEPTH, D_POOL, D_MODEL), D_POOL ** -0.5),
        "w_o": nrm((DEPTH, D_MODEL, D_MODEL), D_MODEL ** -0.5),
        "norm_ffn": gain((DEPTH, D_MODEL)),
        "w_gate": nrm((DEPTH, D_MODEL, D_FF), D_MODEL ** -0.5),
        "w_up": nrm((DEPTH, D_MODEL, D_FF), D_MODEL ** -0.5),
        "w_down": nrm((DEPTH, D_FF, D_MODEL), D_FF ** -0.5),
        "ple_norm": gain((DEPTH, D_MODEL)),
        "w_ple_gate": nrm((DEPTH, D_MODEL, D_MODEL), D_MODEL ** -0.5),
        "w_ple": nrm((DEPTH, D_PLE, D_MODEL), D_PLE ** -0.5),
    }


def reference(x_prompt, x_sample, cache_kv_w128, cache_kv_w512, cache_kv_w2048, state_conv, state_pool,
              p_prompt, p_sample, rel_bias, norm_mix, w_in, conv_w, conv_b, conv_ln_g, conv_ln_b, w_a_out,
              q_norm, k_norm, w_b_out, sgu_ln_g, sgu_ln_b, sgu_w, sgu_b, w_c_out, pool_w, pool_scale,
              w_d_out, w_o, norm_ffn, w_gate, w_up, w_down, ple_norm, w_ple_gate, w_ple):
    caches = (cache_kv_w128, cache_kv_w512, cache_kv_w2048)
    yp, ys = x_prompt, x_sample
    bp = x_prompt.shape[0]
    dt = x_prompt.dtype
    kvp = [[] for _ in DIL_GROUPS]
    kvs = [[] for _ in DIL_GROUPS]
    conv_p, pool_p, conv_s, pool_s, sgu_s = [], [], [], [], []
    for i in range(DEPTH):
        lp = {
            "norm_mix": norm_mix[i], "w_in": w_in[i],
            "conv_w": conv_w[i], "conv_b": conv_b[i],
            "conv_ln_g": conv_ln_g[i], "conv_ln_b": conv_ln_b[i], "w_a_out": w_a_out[i],
            "q_norm": q_norm[i], "k_norm": k_norm[i], "w_b_out": w_b_out[i],
            "sgu_ln_g": sgu_ln_g[i], "sgu_ln_b": sgu_ln_b[i],
            "sgu_w": sgu_w[i], "sgu_b": sgu_b[i], "w_c_out": w_c_out[i],
            "pool_w": pool_w[i], "pool_scale": pool_scale[i], "w_d_out": w_d_out[i],
            "w_o": w_o[i], "norm_ffn": norm_ffn[i],
            "w_gate": w_gate[i], "w_up": w_up[i], "w_down": w_down[i],
            "ple_norm": ple_norm[i], "w_ple_gate": w_ple_gate[i], "w_ple": w_ple[i],
        }
        zero_kv = tuple(jnp.zeros((bp, 0, 2, HEADS_PER_GROUP, HEAD_DIM), dt) for _ in DIL_GROUPS)
        yp, nkv, nconv, npool, _ = _layer(
            yp, p_prompt[i],
            jnp.zeros((bp, CONV_STATE, D_CONV), dt), jnp.zeros((bp, POOL_STATE, D_POOL), dt),
            zero_kv, 0, lp, rel_bias)
        for g in range(len(DIL_GROUPS)):
            kvp[g].append(nkv[g])
        conv_p.append(nconv)
        pool_p.append(npool)
        ys, nkv, nconv, npool, nv = _layer(
            ys, p_sample[i], state_conv[i], state_pool[i],
            tuple(c[i] for c in caches), PAST_LEN, lp, rel_bias)
        for g in range(len(DIL_GROUPS)):
            kvs[g].append(nkv[g])
        conv_s.append(nconv)
        pool_s.append(npool)
        sgu_s.append(nv)
    return (yp, ys,
            jnp.stack(kvp[0]), jnp.stack(kvp[1]), jnp.stack(kvp[2]),
            jnp.stack(conv_p), jnp.stack(pool_p),
            jnp.stack(kvs[0]), jnp.stack(kvs[1]), jnp.stack(kvs[2]),
            jnp.stack(conv_s), jnp.stack(pool_s), jnp.stack(sgu_s))
```

```python
import functools

import numpy as np
import jax
import jax.numpy as jnp
from jax import lax
from jax.experimental import pallas as pl
from jax.experimental.pallas import tpu as pltpu

F32 = jnp.float32
BF16 = jnp.bfloat16

D_MODEL = 1024
PAST_LEN = 16384
D_CONV = 384
CONV_WIDTH = 31
CONV_STATE = CONV_WIDTH - 1
HEAD_DIM = 64
DIL_GROUPS = ((128, 1), (512, 4), (2048, 16))
HEADS_PER_GROUP = 4
N_HEADS = HEADS_PER_GROUP * len(DIL_GROUPS)
D_ATTN = N_HEADS * HEAD_DIM
GROUP_W = HEADS_PER_GROUP * HEAD_DIM
N_BUCKETS = 32
MAX_DISTANCE = 2048
D_SGU = 384
SGU_GROUPS = 4
CHUNK = 128
D_POOL = 384
POOL_WINDOWS = (2, 4, 8, 16)
POOL_STATE = max(POOL_WINDOWS) - 1
POOL_GW = D_POOL // len(POOL_WINDOWS)
D_FF = 2816
D_PLE = 256
N_BRANCH = 4
COL_A = 0
COL_Q = 2 * D_CONV
COL_K = COL_Q + D_ATTN
COL_V = COL_K + D_ATTN
COL_C = COL_V + D_ATTN
COL_D = COL_C + 2 * D_SGU
COL_G = COL_D + D_POOL
D_IN = COL_G + N_BRANCH * D_MODEL
EPS = 1e-6
LN_EPS = 1e-5
NEG_INF = -1e30

Q_BLOCK = 128
CONV_PAD = 32
POOL_PAD = 16
KEY_PAD = 128
V7X_VMEM_LIMIT_BYTES = 56 * 1024 * 1024


def _dot(a, b):
    return jnp.dot(a.astype(BF16), b.astype(BF16), preferred_element_type=F32)


def _dot_nt(a, b):
    return lax.dot_general(a.astype(BF16), b.astype(BF16), (((1,), (1,)), ((), ())),
                           preferred_element_type=F32)


def _rmsnorm(x, g):
    return x * lax.rsqrt(jnp.mean(x * x, axis=-1, keepdims=True) + EPS) * g


def _layernorm(x, g, b):
    mu = jnp.mean(x, axis=-1, keepdims=True)
    xc = x - mu
    return xc * lax.rsqrt(jnp.mean(xc * xc, axis=-1, keepdims=True) + LN_EPS) * g + b


def _sigmoid(x):
    return jax.nn.sigmoid(x)


def _silu(x):
    return x * jax.nn.sigmoid(x)


def _head_rmsnorm(z, head_mean_ref, g):
    ms = _dot(z * z, head_mean_ref[...])
    return z * lax.rsqrt(ms + EPS) * g


def _by_channel_group(vals, n_channels):
    gw = n_channels // len(vals)
    ch = lax.broadcasted_iota(jnp.int32, (1, n_channels), 1)
    out = vals[-1]
    for g in range(len(vals) - 2, -1, -1):
        out = jnp.where(ch < (g + 1) * gw, vals[g], out)
    return out


def _proj_common(x_ref, nmix_ref, win_ref):
    h = _rmsnorm(x_ref[...], nmix_ref[...]).astype(BF16)

    def proj(lo, hi):
        return jnp.dot(h, win_ref[:, lo:hi], preferred_element_type=F32)
    return proj


def _qkv_and_gate1(proj, qn_ref, kn_ref, hm_ref, q_ref, k_ref, v_ref, g1_ref):
    q_ref[...] = _head_rmsnorm(proj(COL_Q, COL_K), hm_ref, qn_ref[...]) * (HEAD_DIM ** -0.5)
    k_ref[...] = _head_rmsnorm(proj(COL_K, COL_V), hm_ref, kn_ref[...])
    v_ref[...] = proj(COL_V, COL_C)
    g1_ref[...] = _sigmoid(proj(COL_G + D_MODEL, COL_G + 2 * D_MODEL))


def _gate(proj, branch):
    lo = COL_G + branch * D_MODEL
    return _sigmoid(proj(lo, lo + D_MODEL))


def _proj_prompt_kernel(x_ref, nmix_ref, win_ref, convw_ref, convb_ref, clng_ref, clnb_ref, wa_ref,
                        qn_ref, kn_ref, hm_ref, slng_ref, slnb_ref, swcat_ref, sbias_ref, wc_ref,
                        pwbd_ref, pscale_ref, wd_ref,
                        pm_ref, g1_ref, q_ref, k_ref, v_ref, cst_ref, pst_ref,
                        cbuf, pbuf, *, tm):
    i = pl.program_id(0)

    @pl.when(i == 0)
    def _():
        cbuf[0:CONV_PAD, :] = jnp.zeros((CONV_PAD, D_CONV), F32)
        pbuf[0:POOL_PAD, :] = jnp.zeros((POOL_PAD, D_POOL), F32)

    proj = _proj_common(x_ref, nmix_ref, win_ref)

    za = proj(COL_A, COL_Q)
    cbuf[CONV_PAD:CONV_PAD + tm, :] = za[:, :D_CONV] * _sigmoid(za[:, D_CONV:])
    acc = jnp.broadcast_to(convb_ref[...], (tm, D_CONV))
    for j in range(CONV_WIDTH):
        acc = acc + convw_ref[j:j + 1, :] * cbuf[pl.ds(j + CONV_PAD - CONV_STATE, tm), :]
    cv = _silu(_layernorm(acc, clng_ref[...], clnb_ref[...]))
    pm = _gate(proj, 0) * _dot(cv, wa_ref[...])
    tail = cbuf[tm:tm + CONV_PAD, :]
    cst_ref[...] = tail
    cbuf[0:CONV_PAD, :] = tail

    zc = proj(COL_C, COL_D)
    u = zc[:, :D_SGU]
    vv = _layernorm(zc[:, D_SGU:], slng_ref[...], slnb_ref[...])
    svs = []
    for c in range(tm // CHUNK):
        r = _dot(swcat_ref[...], vv[c * CHUNK:(c + 1) * CHUNK, :])
        sv = _by_channel_group([r[g * CHUNK:(g + 1) * CHUNK, :] for g in range(SGU_GROUPS)], D_SGU)
        svs.append(sv + sbias_ref[...])
    sv = jnp.concatenate(svs, axis=0) if len(svs) > 1 else svs[0]
    pm = pm + _gate(proj, 2) * _dot(u * sv, wc_ref[...])

    zd = proj(COL_D, COL_G)
    pbuf[POOL_PAD:POOL_PAD + tm, :] = zd
    pos1 = i * tm + lax.broadcasted_iota(jnp.int32, (tm, 1), 0) + 1
    run = jnp.zeros((tm, D_POOL), F32)
    sums, cnts = [], []
    for s in range(max(POOL_WINDOWS)):
        run = run + pbuf[pl.ds(POOL_PAD - s, tm), :]
        if s + 1 in POOL_WINDOWS:
            sums.append(run)
            cnts.append(jnp.minimum(pos1, s + 1).astype(F32))
    pooled = _by_channel_group(sums, D_POOL) / _by_channel_group(cnts, D_POOL) - zd
    mixed = _dot(pooled, pwbd_ref[...]) * pscale_ref[...]
    pm = pm + _gate(proj, 3) * _dot(mixed, wd_ref[...])
    ptail = pbuf[tm:tm + POOL_PAD, :]
    pst_ref[...] = ptail
    pbuf[0:POOL_PAD, :] = ptail

    pm_ref[...] = pm
    _qkv_and_gate1(proj, qn_ref, kn_ref, hm_ref, q_ref, k_ref, v_ref, g1_ref)


def _proj_sample_kernel(x_ref, sconv_ref, spool_ref, nmix_ref, win_ref, convw_ref, convb_ref,
                        clng_ref, clnb_ref, wa_ref, qn_ref, kn_ref, hm_ref, slng_ref, slnb_ref,
                        csgu_ref, sbias_ref, wc_ref, pwbd_ref, pscale_ref, wd_ref,
                        pm_ref, g1_ref, q_ref, k_ref, v_ref, cst_ref, pst_ref, vv_ref,
                        cbuf, pbuf, *, nb, ts, pos0):
    n = nb * ts
    proj = _proj_common(x_ref, nmix_ref, win_ref)

    za = proj(COL_A, COL_Q)
    glu = za[:, :D_CONV] * _sigmoid(za[:, D_CONV:])
    cbuf[:, 0:CONV_PAD, :] = sconv_ref[...]
    cbuf[:, CONV_PAD:CONV_PAD + ts, :] = glu.reshape(nb, ts, D_CONV)
    acc = jnp.broadcast_to(convb_ref[...].reshape(1, 1, D_CONV), (nb, ts, D_CONV))
    for j in range(CONV_WIDTH):
        w = convw_ref[j:j + 1, :].reshape(1, 1, D_CONV)
        acc = acc + w * cbuf[:, pl.ds(j + CONV_PAD - CONV_STATE, ts), :]
    cv = _silu(_layernorm(acc.reshape(n, D_CONV), clng_ref[...], clnb_ref[...]))
    pm = _gate(proj, 0) * _dot(cv, wa_ref[...])
    cst_ref[...] = cbuf[:, ts:ts + CONV_PAD, :]

    zc = proj(COL_C, COL_D)
    u = zc[:, :D_SGU]
    vv = _layernorm(zc[:, D_SGU:], slng_ref[...], slnb_ref[...])
    vv_ref[...] = vv
    vv3 = vv.reshape(nb, ts, D_SGU)
    sv = jnp.broadcast_to(sbias_ref[...].reshape(1, ts, D_SGU), (nb, ts, D_SGU))
    for j in range(ts):
        sv = sv + csgu_ref[j].reshape(1, ts, D_SGU) * vv3[:, j:j + 1, :]
    pm = pm + _gate(proj, 2) * _dot(u * sv.reshape(n, D_SGU), wc_ref[...])

    zd = proj(COL_D, COL_G)
    pbuf[:, 0:POOL_PAD, :] = spool_ref[...]
    pbuf[:, POOL_PAD:POOL_PAD + ts, :] = zd.reshape(nb, ts, D_POOL)
    pos1 = pos0 + lax.broadcasted_iota(jnp.int32, (1, ts, 1), 1) + 1
    run = jnp.zeros((nb, ts, D_POOL), F32)
    sums, cnts = [], []
    for s in range(max(POOL_WINDOWS)):
        run = run + pbuf[:, pl.ds(POOL_PAD - s, ts), :]
        if s + 1 in POOL_WINDOWS:
            sums.append(run)
            cnts.append(jnp.minimum(pos1, s + 1).astype(F32))
    ch = lax.broadcasted_iota(jnp.int32, (1, 1, D_POOL), 2)
    win_sum, cnt = sums[-1], cnts[-1]
    for g in range(len(POOL_WINDOWS) - 2, -1, -1):
        win_sum = jnp.where(ch < (g + 1) * POOL_GW, sums[g], win_sum)
        cnt = jnp.where(ch < (g + 1) * POOL_GW, cnts[g], cnt)
    pooled = (win_sum / cnt).reshape(n, D_POOL) - zd
    mixed = _dot(pooled, pwbd_ref[...]) * pscale_ref[...]
    pm = pm + _gate(proj, 3) * _dot(mixed, wd_ref[...])
    pst_ref[...] = pbuf[:, ts:ts + POOL_PAD, :]

    pm_ref[...] = pm
    _qkv_and_gate1(proj, qn_ref, kn_ref, hm_ref, q_ref, k_ref, v_ref, g1_ref)


def _attn_prompt_kernel(q_ref, kp_ref, kc_ref, vp_ref, vc_ref, bias_ref, o_ref, l_ref):
    first = pl.program_id(1) == 0
    q = q_ref[...].astype(BF16)
    k = jnp.concatenate([kp_ref[...], kc_ref[...]], axis=0).astype(BF16)
    v = jnp.concatenate([vp_ref[...], vc_ref[...]], axis=0).astype(BF16)
    col = lax.broadcasted_iota(jnp.int32, (Q_BLOCK, 2 * Q_BLOCK), 1)
    no_past = jnp.logical_and(col < Q_BLOCK, first)
    for h in range(HEADS_PER_GROUP):
        sl = slice(h * HEAD_DIM, (h + 1) * HEAD_DIM)
        s = _dot_nt(q[:, sl], k[:, sl]) + bias_ref[h]
        s = jnp.where(no_past, NEG_INF, s)
        m = jnp.max(s, axis=-1, keepdims=True)
        p = jnp.exp(s - m)
        den = jnp.sum(p, axis=-1, keepdims=True)
        o_ref[:, sl] = _dot(p, v[:, sl]) / den
        l_ref[:, sl] = jnp.broadcast_to(m + jnp.log(den), (Q_BLOCK, HEAD_DIM))


def _attn_sample_kernel(q_ref, k_ref, v_ref, c1_ref, c2_ref, c3_ref, b1_ref, b2_ref, b3_ref, hmask_ref,
                        o1_ref, o2_ref, o3_ref, l1_ref, l2_ref, l3_ref, n1_ref, n2_ref, n3_ref,
                        e1, e2, e3, *, ts):
    caches = (c1_ref, c2_ref, c3_ref)
    biases = (b1_ref, b2_ref, b3_ref)
    outs = (o1_ref, o2_ref, o3_ref)
    lses = (l1_ref, l2_ref, l3_ref)
    news = (n1_ref, n2_ref, n3_ref)
    exts = (e1, e2, e3)

    @pl.when(pl.program_id(0) == 0)
    def _():
        for ext in exts:
            length = ext.shape[0] - KEY_PAD
            ext[length:length + KEY_PAD, :] = jnp.zeros((KEY_PAD, 2 * GROUP_W), F32)

    for g in range(len(DIL_GROUPS)):
        ext = exts[g]
        length = ext.shape[0] - KEY_PAD
        gs = slice(g * GROUP_W, (g + 1) * GROUP_W)
        ext[0:length, :] = caches[g][0]
        ext[length:length + ts, 0:GROUP_W] = k_ref[:, gs]
        ext[length:length + ts, GROUP_W:2 * GROUP_W] = v_ref[:, gs]
        news[g][0] = ext[ts:ts + length, :]

        qg = q_ref[:, gs]
        qm = jnp.concatenate([qg] * HEADS_PER_GROUP, axis=0) * hmask_ref[...]
        s = _dot_nt(qm, ext[:, 0:GROUP_W]) + biases[g][...]
        m = jnp.max(s, axis=-1, keepdims=True)
        p = jnp.exp(s - m)
        den = jnp.sum(p, axis=-1, keepdims=True)
        o = _dot(p, ext[:, GROUP_W:2 * GROUP_W]) / den
        lse = m + jnp.log(den)
        for h in range(HEADS_PER_GROUP):
            sl = slice(h * HEAD_DIM, (h + 1) * HEAD_DIM)
            rows = slice(h * ts, (h + 1) * ts)
            outs[g][:, sl] = o[rows, sl]
            lses[g][:, sl] = jnp.broadcast_to(lse[rows, :], (ts, HEAD_DIM))


def _mix_kernel(x_ref, pm_ref, g1_ref, o1_ref, o2_ref, o3_ref, l1_ref, l2_ref, l3_ref, pe_ref,
                wb_ref, wo_ref, nffn_ref, wg_ref, wu_ref, wdn_ref, pnorm_ref, wpg_ref, wp_ref, y_ref):
    l1, l2, l3 = l1_ref[...], l2_ref[...], l3_ref[...]
    mx = jnp.maximum(jnp.maximum(l1, l2), l3)
    e1, e2, e3 = jnp.exp(l1 - mx), jnp.exp(l2 - mx), jnp.exp(l3 - mx)
    o = (e1 * o1_ref[...] + e2 * o2_ref[...] + e3 * o3_ref[...]) / (e1 + e2 + e3)
    merged = pm_ref[...] + g1_ref[...] * _dot(o, wb_ref[...])
    x = x_ref[...] + _dot(merged, wo_ref[...])
    h2 = _rmsnorm(x, nffn_ref[...]).astype(BF16)
    ff = _silu(_dot(h2, wg_ref[...])) * _dot(h2, wu_ref[...])
    x = x + _dot(ff, wdn_ref[...])
    h3 = _rmsnorm(x, pnorm_ref[...])
    y_ref[...] = x + _dot(pe_ref[...], wp_ref[...]) * _sigmoid(_dot(h3, wpg_ref[...]))


def _resident(shape):
    nd = len(shape)
    return pl.BlockSpec(shape, lambda *_: (0,) * nd, pipeline_mode=pl.Buffered(1))


def _rows(tm, width):
    return pl.BlockSpec((tm, width), lambda i: (i, 0))


def _params(n_axes):
    return pltpu.CompilerParams(dimension_semantics=("arbitrary",) * n_axes,
                                vmem_limit_bytes=V7X_VMEM_LIMIT_BYTES)


def _proj_weight_args(lw):
    return (lw["nmix"], lw["w_in"], lw["conv_w"], lw["conv_b"], lw["cln_g"], lw["cln_b"], lw["w_a"],
            lw["qn"], lw["kn"], lw["head_mean"], lw["sln_g"], lw["sln_b"])


def _proj_prompt(x, lw, tm):
    t = x.shape[0]
    mid = (lw["sgu_wcat"], lw["sgu_bias"], lw["w_c"], lw["pool_wbd"], lw["pool_scale"], lw["w_d"])
    weights = _proj_weight_args(lw) + mid
    f = lambda w: jax.ShapeDtypeStruct((t, w), F32)
    return pl.pallas_call(
        functools.partial(_proj_prompt_kernel, tm=tm),
        grid=(t // tm,),
        in_specs=[_rows(tm, D_MODEL)] + [_resident(w.shape) for w in weights],
        out_specs=[_rows(tm, D_MODEL), _rows(tm, D_MODEL), _rows(tm, D_ATTN), _rows(tm, D_ATTN),
                   _rows(tm, D_ATTN),
                   pl.BlockSpec((CONV_PAD, D_CONV), lambda i: (0, 0)),
                   pl.BlockSpec((POOL_PAD, D_POOL), lambda i: (0, 0))],
        out_shape=[f(D_MODEL), f(D_MODEL), f(D_ATTN), f(D_ATTN), f(D_ATTN),
                   jax.ShapeDtypeStruct((CONV_PAD, D_CONV), F32),
                   jax.ShapeDtypeStruct((POOL_PAD, D_POOL), F32)],
        scratch_shapes=[pltpu.VMEM((tm + CONV_PAD, D_CONV), F32),
                        pltpu.VMEM((tm + POOL_PAD, D_POOL), F32)],
        compiler_params=_params(1),
        name="proj_prompt",
    )(x, *weights)


def _proj_sample(x, sconv, spool, lw, nb, ts, pos0):
    n = nb * ts
    mid = (lw["sgu_c8"], lw["sgu_bias"][:ts], lw["w_c"], lw["pool_wbd"], lw["pool_scale"], lw["w_d"])
    args = (x, sconv, spool) + _proj_weight_args(lw) + mid
    f = lambda w: jax.ShapeDtypeStruct((n, w), F32)
    full = lambda a: pl.BlockSpec(a.shape, lambda i, nd=a.ndim: (0,) * nd)
    return pl.pallas_call(
        functools.partial(_proj_sample_kernel, nb=nb, ts=ts, pos0=pos0),
        grid=(1,),
        in_specs=[full(a) for a in args],
        out_specs=[pl.BlockSpec((n, D_MODEL), lambda i: (0, 0)), pl.BlockSpec((n, D_MODEL), lambda i: (0, 0)),
                   pl.BlockSpec((n, D_ATTN), lambda i: (0, 0)), pl.BlockSpec((n, D_ATTN), lambda i: (0, 0)),
                   pl.BlockSpec((n, D_ATTN), lambda i: (0, 0)),
                   pl.BlockSpec((nb, CONV_PAD, D_CONV), lambda i: (0, 0, 0)),
                   pl.BlockSpec((nb, POOL_PAD, D_POOL), lambda i: (0, 0, 0)),
                   pl.BlockSpec((n, D_SGU), lambda i: (0, 0))],
        out_shape=[f(D_MODEL), f(D_MODEL), f(D_ATTN), f(D_ATTN), f(D_ATTN),
                   jax.ShapeDtypeStruct((nb, CONV_PAD, D_CONV), F32),
                   jax.ShapeDtypeStruct((nb, POOL_PAD, D_POOL), F32),
                   f(D_SGU)],
        scratch_shapes=[pltpu.VMEM((nb, CONV_PAD + ts, D_CONV), F32),
                        pltpu.VMEM((nb, POOL_PAD + ts, D_POOL), F32)],
        compiler_params=_params(1),
        name="proj_sample",
    )(*args)


def _attn_prompt(q, k, v, bias, g, dil):
    t = q.shape[0]
    rows = t // dil
    cols = D_ATTN // GROUP_W
    qv, kv, vv = (a.reshape(rows, dil * D_ATTN) for a in (q, k, v))
    cur = pl.BlockSpec((Q_BLOCK, GROUP_W), lambda r, u: (u, cols * r + g))
    prev = pl.BlockSpec((Q_BLOCK, GROUP_W), lambda r, u: (jnp.maximum(u - 1, 0), cols * r + g))
    out = pl.BlockSpec((Q_BLOCK, GROUP_W), lambda r, u: (u, r))
    o, lse = pl.pallas_call(
        _attn_prompt_kernel,
        grid=(dil, rows // Q_BLOCK),
        in_specs=[cur, prev, cur, prev, cur,
                  pl.BlockSpec(bias.shape, lambda r, u: (0, 0, 0))],
        out_specs=[out, out],
        out_shape=[jax.ShapeDtypeStruct((rows, dil * GROUP_W), F32)] * 2,
        compiler_params=_params(2),
        name=f"attn_prompt_d{dil}",
    )(qv, kv, kv, vv, vv, bias)
    return o.reshape(t, GROUP_W), lse.reshape(t, GROUP_W)


def _attn_sample(q, k, v, caches, biases, hmask, nb, ts):
    n = nb * ts
    tok = lambda w: pl.BlockSpec((ts, w), lambda b: (b, 0))
    cache_spec = lambda c: pl.BlockSpec((1,) + c.shape[1:], lambda b: (b, 0, 0))
    const = lambda a: pl.BlockSpec(a.shape, lambda b: (0, 0))
    res = pl.pallas_call(
        functools.partial(_attn_sample_kernel, ts=ts),
        grid=(nb,),
        in_specs=[tok(D_ATTN)] * 3 + [cache_spec(c) for c in caches] + [const(b) for b in biases]
                 + [const(hmask)],
        out_specs=[tok(GROUP_W)] * 6 + [cache_spec(c) for c in caches],
        out_shape=[jax.ShapeDtypeStruct((n, GROUP_W), F32)] * 6
                  + [jax.ShapeDtypeStruct(c.shape, F32) for c in caches],
        scratch_shapes=[pltpu.VMEM((c.shape[1] + KEY_PAD, 2 * GROUP_W), F32) for c in caches],
        compiler_params=_params(1),
        name="attn_sample",
    )(q, k, v, *caches, *biases, hmask)
    return res[0:3], res[3:6], res[6:9]


def _mix(x, pm, g1, os_, ls_, pe, lw, tm):
    t = x.shape[0]
    weights = (lw["w_b"], lw["w_o"], lw["nffn"], lw["w_gate"], lw["w_up"], lw["w_down"],
               lw["pnorm"], lw["w_pg"], lw["w_p"])
    return pl.pallas_call(
        _mix_kernel,
        grid=(t // tm,),
        in_specs=[_rows(tm, D_MODEL)] * 3 + [_rows(tm, GROUP_W)] * 6 + [_rows(tm, D_PLE)]
                 + [_resident(w.shape) for w in weights],
        out_specs=_rows(tm, D_MODEL),
        out_shape=jax.ShapeDtypeStruct((t, D_MODEL), F32),
        compiler_params=_params(1),
        name="mix",
    )(x, pm, g1, *os_, *ls_, pe, *weights)


def _t5_bucket(dist):
    max_exact = N_BUCKETS // 2
    d = np.asarray(dist)
    large = max_exact + (np.log(np.maximum(d, 1) / max_exact)
                         / np.log(MAX_DISTANCE / max_exact)
                         * (N_BUCKETS - max_exact)).astype(np.int64)
    large = np.minimum(large, N_BUCKETS - 1)
    return np.where(d < max_exact, d, large).astype(np.int32)


def _prompt_bias(rel_bias, g, win, dil):
    i = np.arange(Q_BLOCK)[:, None]
    c = np.arange(2 * Q_BLOCK)[None, :]
    j = i + Q_BLOCK - c
    valid = (j >= 0) & (j <= win // dil)
    bucket = _t5_bucket(dil * np.clip(j, 0, win // dil))
    b = rel_bias[bucket][:, :, g * HEADS_PER_GROUP:(g + 1) * HEADS_PER_GROUP]
    return jnp.where(valid[None], jnp.transpose(b, (2, 0, 1)), NEG_INF).astype(F32)


def _sample_bias(rel_bias, g, win, dil, length, ts):
    t = np.arange(ts)[:, None]
    c = np.arange(length + KEY_PAD)[None, :]
    diff = length + t - c
    valid = (diff >= 0) & (diff <= win) & (diff % dil == 0)
    bucket = _t5_bucket(np.clip(diff, 0, win))
    b = rel_bias[bucket][:, :, g * HEADS_PER_GROUP:(g + 1) * HEADS_PER_GROUP]
    b = jnp.where(valid[:, :, None], b, NEG_INF)
    return jnp.transpose(b, (2, 0, 1)).reshape(HEADS_PER_GROUP * ts, length + KEY_PAD).astype(F32)


def _layer_weights(i, ts, norm_mix, w_in, conv_w, conv_b, conv_ln_g, conv_ln_b, w_a_out, q_norm, k_norm,
                   w_b_out, sgu_ln_g, sgu_ln_b, sgu_w, sgu_b, w_c_out, pool_w, pool_scale, w_d_out,
                   w_o, norm_ffn, w_gate, w_up, w_down, ple_norm, w_ple_gate, w_ple):
    row = lambda a: a[i].reshape(1, -1).astype(F32)
    bf = lambda a: a[i].astype(BF16)
    tril = np.tril(np.ones((CHUNK, CHUNK), dtype=bool))
    ws = jnp.where(tril[None], sgu_w[i], 0.0)
    gw = D_SGU // SGU_GROUPS
    pool_bd = jnp.zeros((D_POOL, D_POOL), F32)
    for g in range(len(POOL_WINDOWS)):
        pool_bd = lax.dynamic_update_slice(pool_bd, pool_w[i, g], (g * POOL_GW, g * POOL_GW))
    head_mean = np.kron(np.eye(N_HEADS), np.full((HEAD_DIM, HEAD_DIM), 1.0 / HEAD_DIM))
    return {
        "nmix": row(norm_mix), "w_in": bf(w_in), "conv_w": conv_w[i], "conv_b": row(conv_b),
        "cln_g": row(conv_ln_g), "cln_b": row(conv_ln_b), "w_a": bf(w_a_out),
        "qn": jnp.tile(q_norm[i], N_HEADS).reshape(1, D_ATTN),
        "kn": jnp.tile(k_norm[i], N_HEADS).reshape(1, D_ATTN),
        "head_mean": jnp.asarray(head_mean, BF16),
        "sln_g": row(sgu_ln_g), "sln_b": row(sgu_ln_b),
        "sgu_wcat": ws.reshape(SGU_GROUPS * CHUNK, CHUNK).astype(BF16),
        "sgu_c8": jnp.repeat(jnp.transpose(ws[:, :ts, :ts], (2, 1, 0)), gw, axis=2),
        "sgu_bias": jnp.repeat(sgu_b[i].T, gw, axis=1),
        "w_c": bf(w_c_out), "pool_wbd": pool_bd.astype(BF16), "pool_scale": row(pool_scale),
        "w_d": bf(w_d_out), "w_b": bf(w_b_out), "w_o": bf(w_o), "nffn": row(norm_ffn),
        "w_gate": bf(w_gate), "w_up": bf(w_up), "w_down": bf(w_down), "pnorm": row(ple_norm),
        "w_pg": bf(w_ple_gate), "w_p": bf(w_ple),
    }


def _kv_rows(k, v, g, keep):
    gs = slice(g * GROUP_W, (g + 1) * GROUP_W)
    t = k.shape[0]
    kk = k[t - keep:, gs].reshape(keep, HEADS_PER_GROUP, HEAD_DIM)
    vv = v[t - keep:, gs].reshape(keep, HEADS_PER_GROUP, HEAD_DIM)
    return jnp.stack([kk, vv], axis=1)


def kernel(x_prompt, x_sample, cache_kv_w128, cache_kv_w512, cache_kv_w2048, state_conv, state_pool, p_prompt, p_sample, rel_bias, norm_mix, w_in, conv_w, conv_b, conv_ln_g, conv_ln_b, w_a_out, q_norm, k_norm, w_b_out, sgu_ln_g, sgu_ln_b, sgu_w, sgu_b, w_c_out, pool_w, pool_scale, w_d_out, w_o, norm_ffn, w_gate, w_up, w_down, ple_norm, w_ple_gate, w_ple):
    bp, seq, _ = x_prompt.shape
    nb, ts, _ = x_sample.shape
    depth = w_in.shape[0]
    caches = (cache_kv_w128, cache_kv_w512, cache_kv_w2048)
    assert bp == 1 and seq % (max(d for _, d in DIL_GROUPS) * Q_BLOCK) == 0
    assert ts % 8 == 0 and ts <= CHUNK
    for c, (win, _) in zip(caches, DIL_GROUPS):
        assert c.shape[2] == win, "sample caches must hold a full window"
    tm = 256

    prompt_bias = [_prompt_bias(rel_bias, g, win, dil) for g, (win, dil) in enumerate(DIL_GROUPS)]
    sample_bias = [_sample_bias(rel_bias, g, win, dil, win, ts) for g, (win, dil) in enumerate(DIL_GROUPS)]
    hmask = jnp.asarray(np.kron(np.eye(HEADS_PER_GROUP), np.ones((ts, HEAD_DIM))), F32)

    yp = x_prompt.reshape(seq, D_MODEL)
    ys = x_sample.reshape(nb * ts, D_MODEL)
    kvp = [[] for _ in DIL_GROUPS]
    kvs = [[] for _ in DIL_GROUPS]
    conv_p, pool_p, conv_s, pool_s, sgu_s = [], [], [], [], []
    for i in range(depth):
        lw = _layer_weights(i, ts, norm_mix, w_in, conv_w, conv_b, conv_ln_g, conv_ln_b, w_a_out, q_norm,
                            k_norm, w_b_out, sgu_ln_g, sgu_ln_b, sgu_w, sgu_b, w_c_out, pool_w, pool_scale,
                            w_d_out, w_o, norm_ffn, w_gate, w_up, w_down, ple_norm, w_ple_gate, w_ple)

        pm, g1, q, k, v, cst, pst = _proj_prompt(yp, lw, tm)
        os_, ls_ = [], []
        for g, (win, dil) in enumerate(DIL_GROUPS):
            o, lse = _attn_prompt(q, k, v, prompt_bias[g], g, dil)
            os_.append(o)
            ls_.append(lse)
            kvp[g].append(_kv_rows(k, v, g, min(win, seq))[None])
        yp = _mix(yp, pm, g1, os_, ls_, p_prompt[i, 0], lw, tm)
        conv_p.append(cst[CONV_PAD - CONV_STATE:][None])
        pool_p.append(pst[POOL_PAD - POOL_STATE:][None])

        sconv = jnp.pad(state_conv[i], ((0, 0), (CONV_PAD - CONV_STATE, 0), (0, 0)))
        spool = jnp.pad(state_pool[i], ((0, 0), (POOL_PAD - POOL_STATE, 0), (0, 0)))
        pm, g1, q, k, v, cst, pst, vv = _proj_sample(ys, sconv, spool, lw, nb, ts, PAST_LEN)
        flat = [c[i].reshape(nb, c.shape[2], 2 * GROUP_W) for c in caches]
        os_, ls_, new = _attn_sample(q, k, v, flat, sample_bias, hmask, nb, ts)
        ys = _mix(ys, pm, g1, os_, ls_, p_sample[i].reshape(nb * ts, D_PLE), lw, nb * ts)
        for g in range(len(DIL_GROUPS)):
            kvs[g].append(new[g].reshape(nb, -1, 2, HEADS_PER_GROUP, HEAD_DIM))
        conv_s.append(cst[:, CONV_PAD - CONV_STATE:])
        pool_s.append(pst[:, POOL_PAD - POOL_STATE:])
        sgu_s.append(vv.reshape(nb, ts, D_SGU))

    return (yp.reshape(bp, seq, D_MODEL), ys.reshape(nb, ts, D_MODEL),
            jnp.stack(kvp[0]), jnp.stack(kvp[1]), jnp.stack(kvp[2]),
            jnp.stack(conv_p), jnp.stack(pool_p),
            jnp.stack(kvs[0]), jnp.stack(kvs[1]), jnp.stack(kvs[2]),
            jnp.stack(conv_s), jnp.stack(pool_s), jnp.stack(sgu_s))
```

```python
import functools

import numpy as np
import jax
import jax.numpy as jnp
from jax import lax
from jax.experimental import pallas as pl
from jax.experimental.pallas import tpu as pltpu

F32 = jnp.float32
BF16 = jnp.bfloat16

D_MODEL = 1024
PAST_LEN = 16384
D_CONV = 384
CONV_WIDTH = 31
CONV_STATE = CONV_WIDTH - 1
HEAD_DIM = 64
DIL_GROUPS = ((128, 1), (512, 4), (2048, 16))
HEADS_PER_GROUP = 4
N_HEADS = HEADS_PER_GROUP * len(DIL_GROUPS)
D_ATTN = N_HEADS * HEAD_DIM
GROUP_W = HEADS_PER_GROUP * HEAD_DIM
N_BUCKETS = 32
MAX_DISTANCE = 2048
D_SGU = 384
SGU_GROUPS = 4
CHUNK = 128
D_POOL = 384
POOL_WINDOWS = (2, 4, 8, 16)
POOL_STATE = max(POOL_WINDOWS) - 1
POOL_GW = D_POOL // len(POOL_WINDOWS)
D_FF = 2816
D_PLE = 256
N_BRANCH = 4
COL_A = 0
COL_Q = 2 * D_CONV
COL_K = COL_Q + D_ATTN
COL_V = COL_K + D_ATTN
COL_C = COL_V + D_ATTN
COL_D = COL_C + 2 * D_SGU
COL_G = COL_D + D_POOL
D_IN = COL_G + N_BRANCH * D_MODEL
EPS = 1e-6
LN_EPS = 1e-5
NEG_INF = -1e30

Q_BLOCK = 128
ATTN_TOKEN_BLOCK = 2048
CONV_PAD = 32
POOL_PAD = 16
KEY_PAD = 128
V7X_VMEM_LIMIT_BYTES = 56 * 1024 * 1024


def _dot(a, b):
    return jnp.dot(a.astype(BF16), b.astype(BF16), preferred_element_type=F32)


def _dot_nt(a, b):
    return lax.dot_general(a.astype(BF16), b.astype(BF16), (((1,), (1,)), ((), ())),
                           preferred_element_type=F32)


def _rmsnorm(x, g):
    return x * lax.rsqrt(jnp.mean(x * x, axis=-1, keepdims=True) + EPS) * g


def _layernorm(x, g, b):
    mu = jnp.mean(x, axis=-1, keepdims=True)
    xc = x - mu
    return xc * lax.rsqrt(jnp.mean(xc * xc, axis=-1, keepdims=True) + LN_EPS) * g + b


def _sigmoid(x):
    return jax.nn.sigmoid(x)


def _silu(x):
    return x * jax.nn.sigmoid(x)


def _head_rmsnorm(z, head_mean_ref, g):
    ms = _dot(z * z, head_mean_ref[...])
    return z * lax.rsqrt(ms + EPS) * g


def _by_channel_group(vals, n_channels):
    gw = n_channels // len(vals)
    ch = lax.broadcasted_iota(jnp.int32, (1, n_channels), 1)
    out = vals[-1]
    for g in range(len(vals) - 2, -1, -1):
        out = jnp.where(ch < (g + 1) * gw, vals[g], out)
    return out


def _proj_common(x_ref, nmix_ref, win_ref):
    h = _rmsnorm(x_ref[...], nmix_ref[...]).astype(BF16)

    def proj(lo, hi):
        return jnp.dot(h, win_ref[:, lo:hi], preferred_element_type=F32)
    return proj


def _qkv_and_gate1(proj, qn_ref, kn_ref, hm_ref, q_ref, k_ref, v_ref, g1_ref):
    q_ref[...] = _head_rmsnorm(proj(COL_Q, COL_K), hm_ref, qn_ref[...]) * (HEAD_DIM ** -0.5)
    k_ref[...] = _head_rmsnorm(proj(COL_K, COL_V), hm_ref, kn_ref[...])
    v_ref[...] = proj(COL_V, COL_C)
    g1_ref[...] = _sigmoid(proj(COL_G + D_MODEL, COL_G + 2 * D_MODEL))


def _gate(proj, branch):
    lo = COL_G + branch * D_MODEL
    return _sigmoid(proj(lo, lo + D_MODEL))


def _proj_prompt_kernel(x_ref, nmix_ref, win_ref, convw_ref, convb_ref, clng_ref, clnb_ref, wa_ref,
                        qn_ref, kn_ref, hm_ref, slng_ref, slnb_ref, swcat_ref, sbias_ref, wc_ref,
                        pwbd_ref, pscale_ref, wd_ref,
                        pm_ref, g1_ref, q_ref, k_ref, v_ref, cst_ref, pst_ref,
                        cbuf, pbuf, *, tm):
    i = pl.program_id(0)

    @pl.when(i == 0)
    def _():
        cbuf[0:CONV_PAD, :] = jnp.zeros((CONV_PAD, D_CONV), F32)
        pbuf[0:POOL_PAD, :] = jnp.zeros((POOL_PAD, D_POOL), F32)

    proj = _proj_common(x_ref, nmix_ref, win_ref)

    za = proj(COL_A, COL_Q)
    cbuf[CONV_PAD:CONV_PAD + tm, :] = za[:, :D_CONV] * _sigmoid(za[:, D_CONV:])
    acc = jnp.broadcast_to(convb_ref[...], (tm, D_CONV))
    for j in range(CONV_WIDTH):
        acc = acc + convw_ref[j:j + 1, :] * cbuf[pl.ds(j + CONV_PAD - CONV_STATE, tm), :]
    cv = _silu(_layernorm(acc, clng_ref[...], clnb_ref[...]))
    pm = _gate(proj, 0) * _dot(cv, wa_ref[...])
    tail = cbuf[tm:tm + CONV_PAD, :]
    cst_ref[...] = tail
    cbuf[0:CONV_PAD, :] = tail

    zc = proj(COL_C, COL_D)
    u = zc[:, :D_SGU]
    vv = _layernorm(zc[:, D_SGU:], slng_ref[...], slnb_ref[...])
    svs = []
    for c in range(tm // CHUNK):
        r = _dot(swcat_ref[...], vv[c * CHUNK:(c + 1) * CHUNK, :])
        sv = _by_channel_group([r[g * CHUNK:(g + 1) * CHUNK, :] for g in range(SGU_GROUPS)], D_SGU)
        svs.append(sv + sbias_ref[...])
    sv = jnp.concatenate(svs, axis=0) if len(svs) > 1 else svs[0]
    pm = pm + _gate(proj, 2) * _dot(u * sv, wc_ref[...])

    zd = proj(COL_D, COL_G)
    pbuf[POOL_PAD:POOL_PAD + tm, :] = zd
    pos1 = i * tm + lax.broadcasted_iota(jnp.int32, (tm, 1), 0) + 1
    run = jnp.zeros((tm, D_POOL), F32)
    sums, cnts = [], []
    for s in range(max(POOL_WINDOWS)):
        run = run + pbuf[pl.ds(POOL_PAD - s, tm), :]
        if s + 1 in POOL_WINDOWS:
            sums.append(run)
            cnts.append(jnp.minimum(pos1, s + 1).astype(F32))
    pooled = _by_channel_group(sums, D_POOL) / _by_channel_group(cnts, D_POOL) - zd
    mixed = _dot(pooled, pwbd_ref[...]) * pscale_ref[...]
    pm = pm + _gate(proj, 3) * _dot(mixed, wd_ref[...])
    ptail = pbuf[tm:tm + POOL_PAD, :]
    pst_ref[...] = ptail
    pbuf[0:POOL_PAD, :] = ptail

    pm_ref[...] = pm
    _qkv_and_gate1(proj, qn_ref, kn_ref, hm_ref, q_ref, k_ref, v_ref, g1_ref)


def _proj_sample_kernel(x_ref, sconv_ref, spool_ref, nmix_ref, win_ref, convw_ref, convb_ref,
                        clng_ref, clnb_ref, wa_ref, qn_ref, kn_ref, hm_ref, slng_ref, slnb_ref,
                        csgu_ref, sbias_ref, wc_ref, pwbd_ref, pscale_ref, wd_ref,
                        pm_ref, g1_ref, q_ref, k_ref, v_ref, cst_ref, pst_ref, vv_ref,
                        cbuf, pbuf, *, nb, ts, pos0):
    n = nb * ts
    proj = _proj_common(x_ref, nmix_ref, win_ref)

    za = proj(COL_A, COL_Q)
    glu = za[:, :D_CONV] * _sigmoid(za[:, D_CONV:])
    cbuf[:, 0:CONV_PAD, :] = sconv_ref[...]
    cbuf[:, CONV_PAD:CONV_PAD + ts, :] = glu.reshape(nb, ts, D_CONV)
    acc = jnp.broadcast_to(convb_ref[...].reshape(1, 1, D_CONV), (nb, ts, D_CONV))
    for j in range(CONV_WIDTH):
        w = convw_ref[j:j + 1, :].reshape(1, 1, D_CONV)
        acc = acc + w * cbuf[:, pl.ds(j + CONV_PAD - CONV_STATE, ts), :]
    cv = _silu(_layernorm(acc.reshape(n, D_CONV), clng_ref[...], clnb_ref[...]))
    pm = _gate(proj, 0) * _dot(cv, wa_ref[...])
    cst_ref[...] = cbuf[:, ts:ts + CONV_PAD, :]

    zc = proj(COL_C, COL_D)
    u = zc[:, :D_SGU]
    vv = _layernorm(zc[:, D_SGU:], slng_ref[...], slnb_ref[...])
    vv_ref[...] = vv
    vv3 = vv.reshape(nb, ts, D_SGU)
    sv = jnp.broadcast_to(sbias_ref[...].reshape(1, ts, D_SGU), (nb, ts, D_SGU))
    for j in range(ts):
        sv = sv + csgu_ref[j].reshape(1, ts, D_SGU) * vv3[:, j:j + 1, :]
    pm = pm + _gate(proj, 2) * _dot(u * sv.reshape(n, D_SGU), wc_ref[...])

    zd = proj(COL_D, COL_G)
    pbuf[:, 0:POOL_PAD, :] = spool_ref[...]
    pbuf[:, POOL_PAD:POOL_PAD + ts, :] = zd.reshape(nb, ts, D_POOL)
    pos1 = pos0 + lax.broadcasted_iota(jnp.int32, (1, ts, 1), 1) + 1
    run = jnp.zeros((nb, ts, D_POOL), F32)
    sums, cnts = [], []
    for s in range(max(POOL_WINDOWS)):
        run = run + pbuf[:, pl.ds(POOL_PAD - s, ts), :]
        if s + 1 in POOL_WINDOWS:
            sums.append(run)
            cnts.append(jnp.minimum(pos1, s + 1).astype(F32))
    ch = lax.broadcasted_iota(jnp.int32, (1, 1, D_POOL), 2)
    win_sum, cnt = sums[-1], cnts[-1]
    for g in range(len(POOL_WINDOWS) - 2, -1, -1):
        win_sum = jnp.where(ch < (g + 1) * POOL_GW, sums[g], win_sum)
        cnt = jnp.where(ch < (g + 1) * POOL_GW, cnts[g], cnt)
    pooled = (win_sum / cnt).reshape(n, D_POOL) - zd
    mixed = _dot(pooled, pwbd_ref[...]) * pscale_ref[...]
    pm = pm + _gate(proj, 3) * _dot(mixed, wd_ref[...])
    pst_ref[...] = pbuf[:, ts:ts + POOL_PAD, :]

    pm_ref[...] = pm
    _qkv_and_gate1(proj, qn_ref, kn_ref, hm_ref, q_ref, k_ref, v_ref, g1_ref)


def _class_rows(start, dil):
    return pl.ds(start, Q_BLOCK, stride=dil) if dil > 1 else pl.ds(start, Q_BLOCK)


def _attn_prompt_kernel(q_ref, hk_ref, kc_ref, hv_ref, vc_ref, bias_ref, o_ref, l_ref, *, dil):
    tb = q_ref.shape[0]
    n_sub = tb // (dil * Q_BLOCK)
    first = pl.program_id(0) == 0
    col = lax.broadcasted_iota(jnp.int32, (Q_BLOCK, 2 * Q_BLOCK), 1)
    no_past = jnp.logical_and(col < Q_BLOCK, first)
    for r in range(dil):
        for j in range(n_sub):
            rows = _class_rows(r + dil * Q_BLOCK * j, dil)
            if j == 0:
                kp, vp = hk_ref[_class_rows(r, dil), :], hv_ref[_class_rows(r, dil), :]
            else:
                prev = _class_rows(r + dil * Q_BLOCK * (j - 1), dil)
                kp, vp = kc_ref[prev, :], vc_ref[prev, :]
            q = q_ref[rows, :].astype(BF16)
            k = jnp.concatenate([kp, kc_ref[rows, :]], axis=0).astype(BF16)
            v = jnp.concatenate([vp, vc_ref[rows, :]], axis=0).astype(BF16)
            outs, lses = [], []
            for h in range(2):
                sl = slice(h * HEAD_DIM, (h + 1) * HEAD_DIM)
                s = _dot_nt(q[:, sl], k[:, sl]) + bias_ref[h]
                if j == 0:
                    s = jnp.where(no_past, NEG_INF, s)
                m = jnp.max(s, axis=-1, keepdims=True)
                p = jnp.exp(s - m)
                den = jnp.sum(p, axis=-1, keepdims=True)
                outs.append(_dot(p, v[:, sl]) / den)
                lses.append(jnp.broadcast_to(m + jnp.log(den), (Q_BLOCK, HEAD_DIM)))
            o_ref[rows, :] = jnp.concatenate(outs, axis=1)
            l_ref[rows, :] = jnp.concatenate(lses, axis=1)


def _attn_sample_kernel(*refs, ts, aliased):
    n_g = len(DIL_GROUPS)
    q_ref, k_ref, v_ref = refs[0:3]
    caches = refs[3:3 + n_g]
    biases = refs[3 + n_g:3 + 2 * n_g]
    hmask_ref = refs[3 + 2 * n_g]
    pos = 4 + 2 * n_g + (n_g if aliased else 0)
    outs = refs[pos:pos + n_g]
    lses = refs[pos + n_g:pos + 2 * n_g]
    news = refs[pos + 2 * n_g:pos + 3 * n_g]
    exts = refs[pos + 3 * n_g:pos + 4 * n_g]

    for g in range(n_g):
        ext = exts[g]
        length = ext.shape[2] - KEY_PAD
        gs = slice(g * GROUP_W, (g + 1) * GROUP_W)
        zeros = jnp.zeros((KEY_PAD - ts, GROUP_W), F32)
        for kv, new_ref in enumerate((k_ref, v_ref)):
            ext[kv, :, 0:length] = caches[g][0, 0, kv]
            ext[kv, :, length:length + KEY_PAD] = jnp.concatenate([new_ref[:, gs], zeros], axis=0).T
            news[g][0, 0, kv] = ext[kv][:, ts:ts + length]

        qm = jnp.concatenate([q_ref[:, gs]] * HEADS_PER_GROUP, axis=0) * hmask_ref[...]
        s = _dot(qm, ext[0]) + biases[g][...]
        m = jnp.max(s, axis=-1, keepdims=True)
        p = jnp.exp(s - m)
        den = jnp.sum(p, axis=-1, keepdims=True)
        o = _dot_nt(p, ext[1]) / den
        lse = m + jnp.log(den)
        for h in range(HEADS_PER_GROUP):
            sl = slice(h * HEAD_DIM, (h + 1) * HEAD_DIM)
            rows = slice(h * ts, (h + 1) * ts)
            outs[g][:, sl] = o[rows, sl]
            lses[g][:, sl] = jnp.broadcast_to(lse[rows, :], (ts, HEAD_DIM))


def _mix_kernel(x_ref, pm_ref, g1_ref, o1_ref, o2_ref, o3_ref, l1_ref, l2_ref, l3_ref, pe_ref,
                wb_ref, wo_ref, nffn_ref, wg_ref, wu_ref, wdn_ref, pnorm_ref, wpg_ref, wp_ref, y_ref):
    l1, l2, l3 = l1_ref[...], l2_ref[...], l3_ref[...]
    mx = jnp.maximum(jnp.maximum(l1, l2), l3)
    e1, e2, e3 = jnp.exp(l1 - mx), jnp.exp(l2 - mx), jnp.exp(l3 - mx)
    o = (e1 * o1_ref[...] + e2 * o2_ref[...] + e3 * o3_ref[...]) / (e1 + e2 + e3)
    merged = pm_ref[...] + g1_ref[...] * _dot(o, wb_ref[...])
    x = x_ref[...] + _dot(merged, wo_ref[...])
    h2 = _rmsnorm(x, nffn_ref[...]).astype(BF16)
    ff = _silu(_dot(h2, wg_ref[...])) * _dot(h2, wu_ref[...])
    x = x + _dot(ff, wdn_ref[...])
    h3 = _rmsnorm(x, pnorm_ref[...])
    y_ref[...] = x + _dot(pe_ref[...], wp_ref[...]) * _sigmoid(_dot(h3, wpg_ref[...]))


def _resident(shape):
    nd = len(shape)
    return pl.BlockSpec(shape, lambda *_: (0,) * nd, pipeline_mode=pl.Buffered(1))


def _rows(tm, width):
    return pl.BlockSpec((tm, width), lambda i: (i, 0))


def _params(n_axes):
    return pltpu.CompilerParams(dimension_semantics=("arbitrary",) * n_axes,
                                vmem_limit_bytes=V7X_VMEM_LIMIT_BYTES)


def _proj_weight_args(lw):
    return (lw["nmix"], lw["w_in"], lw["conv_w"], lw["conv_b"], lw["cln_g"], lw["cln_b"], lw["w_a"],
            lw["qn"], lw["kn"], lw["head_mean"], lw["sln_g"], lw["sln_b"])


def _proj_prompt(x, lw, tm):
    t = x.shape[0]
    mid = (lw["sgu_wcat"], lw["sgu_bias"], lw["w_c"], lw["pool_wbd"], lw["pool_scale"], lw["w_d"])
    weights = _proj_weight_args(lw) + mid
    f = lambda w: jax.ShapeDtypeStruct((t, w), F32)
    return pl.pallas_call(
        functools.partial(_proj_prompt_kernel, tm=tm),
        grid=(t // tm,),
        in_specs=[_rows(tm, D_MODEL)] + [_resident(w.shape) for w in weights],
        out_specs=[_rows(tm, D_MODEL), _rows(tm, D_MODEL), _rows(tm, D_ATTN), _rows(tm, D_ATTN),
                   _rows(tm, D_ATTN),
                   pl.BlockSpec((CONV_PAD, D_CONV), lambda i: (0, 0)),
                   pl.BlockSpec((POOL_PAD, D_POOL), lambda i: (0, 0))],
        out_shape=[f(D_MODEL), f(D_MODEL), f(D_ATTN), f(D_ATTN), f(D_ATTN),
                   jax.ShapeDtypeStruct((CONV_PAD, D_CONV), F32),
                   jax.ShapeDtypeStruct((POOL_PAD, D_POOL), F32)],
        scratch_shapes=[pltpu.VMEM((tm + CONV_PAD, D_CONV), F32),
                        pltpu.VMEM((tm + POOL_PAD, D_POOL), F32)],
        compiler_params=_params(1),
        name="proj_prompt",
    )(x, *weights)


def _proj_sample(x, sconv, spool, lw, nb, ts, pos0):
    n = nb * ts
    mid = (lw["sgu_c8"], lw["sgu_bias"][:ts], lw["w_c"], lw["pool_wbd"], lw["pool_scale"], lw["w_d"])
    args = (x, sconv, spool) + _proj_weight_args(lw) + mid
    f = lambda w: jax.ShapeDtypeStruct((n, w), F32)
    full = lambda a: pl.BlockSpec(a.shape, lambda i, nd=a.ndim: (0,) * nd)
    return pl.pallas_call(
        functools.partial(_proj_sample_kernel, nb=nb, ts=ts, pos0=pos0),
        grid=(1,),
        in_specs=[full(a) for a in args],
        out_specs=[pl.BlockSpec((n, D_MODEL), lambda i: (0, 0)), pl.BlockSpec((n, D_MODEL), lambda i: (0, 0)),
                   pl.BlockSpec((n, D_ATTN), lambda i: (0, 0)), pl.BlockSpec((n, D_ATTN), lambda i: (0, 0)),
                   pl.BlockSpec((n, D_ATTN), lambda i: (0, 0)),
                   pl.BlockSpec((nb, CONV_PAD, D_CONV), lambda i: (0, 0, 0)),
                   pl.BlockSpec((nb, POOL_PAD, D_POOL), lambda i: (0, 0, 0)),
                   pl.BlockSpec((n, D_SGU), lambda i: (0, 0))],
        out_shape=[f(D_MODEL), f(D_MODEL), f(D_ATTN), f(D_ATTN), f(D_ATTN),
                   jax.ShapeDtypeStruct((nb, CONV_PAD, D_CONV), F32),
                   jax.ShapeDtypeStruct((nb, POOL_PAD, D_POOL), F32),
                   f(D_SGU)],
        scratch_shapes=[pltpu.VMEM((nb, CONV_PAD + ts, D_CONV), F32),
                        pltpu.VMEM((nb, POOL_PAD + ts, D_POOL), F32)],
        compiler_params=_params(1),
        name="proj_sample",
    )(*args)


def _attn_prompt(q, k, v, bias, g, dil):
    t = q.shape[0]
    tb = ATTN_TOKEN_BLOCK
    hb = dil * Q_BLOCK
    lane_blocks = GROUP_W // 128
    cur = pl.BlockSpec((tb, 128), lambda b, hp: (b, lane_blocks * g + hp))
    halo = pl.BlockSpec((hb, 128), lambda b, hp: (jnp.maximum(b * (tb // hb) - 1, 0), lane_blocks * g + hp))
    out = pl.BlockSpec((tb, 128), lambda b, hp: (b, hp))
    o, lse = pl.pallas_call(
        functools.partial(_attn_prompt_kernel, dil=dil),
        grid=(t // tb, lane_blocks),
        in_specs=[cur, halo, cur, halo, cur,
                  pl.BlockSpec((2, Q_BLOCK, 2 * Q_BLOCK), lambda b, hp: (hp, 0, 0))],
        out_specs=[out, out],
        out_shape=[jax.ShapeDtypeStruct((t, GROUP_W), F32)] * 2,
        compiler_params=_params(2),
        name=f"attn_prompt_d{dil}",
    )(q, k, k, v, v, bias)
    return o, lse


def _attn_sample(layer, q, k, v, caches_t, new_prev, biases, hmask, nb, ts):
    n = nb * ts
    aliased = new_prev is not None
    tok = lambda w: pl.BlockSpec((ts, w), lambda b: (b, 0))
    cache_spec = lambda c: pl.BlockSpec((1, 1) + c.shape[2:], lambda b: (layer, b, 0, 0, 0))
    const = lambda a: pl.BlockSpec(a.shape, lambda b: (0, 0))
    n_in = 3 + 2 * len(caches_t) + 1
    res = pl.pallas_call(
        functools.partial(_attn_sample_kernel, ts=ts, aliased=aliased),
        grid=(nb,),
        in_specs=[tok(D_ATTN)] * 3 + [cache_spec(c) for c in caches_t] + [const(b) for b in biases]
                 + [const(hmask)]
                 + ([pl.BlockSpec(memory_space=pl.ANY)] * len(caches_t) if aliased else []),
        out_specs=[tok(GROUP_W)] * 6 + [cache_spec(c) for c in caches_t],
        out_shape=[jax.ShapeDtypeStruct((n, GROUP_W), F32)] * 6
                  + [jax.ShapeDtypeStruct(c.shape, F32) for c in caches_t],
        scratch_shapes=[pltpu.VMEM((2, GROUP_W, c.shape[4] + KEY_PAD), F32) for c in caches_t],
        input_output_aliases={n_in + i: 6 + i for i in range(len(caches_t))} if aliased else {},
        compiler_params=_params(1),
        name="attn_sample",
    )(q, k, v, *caches_t, *biases, hmask, *(new_prev if aliased else ()))
    return res[0:3], res[3:6], res[6:9]


def _mix(x, pm, g1, os_, ls_, pe, lw, tm):
    t = x.shape[0]
    weights = (lw["w_b"], lw["w_o"], lw["nffn"], lw["w_gate"], lw["w_up"], lw["w_down"],
               lw["pnorm"], lw["w_pg"], lw["w_p"])
    return pl.pallas_call(
        _mix_kernel,
        grid=(t // tm,),
        in_specs=[_rows(tm, D_MODEL)] * 3 + [_rows(tm, GROUP_W)] * 6 + [_rows(tm, D_PLE)]
                 + [_resident(w.shape) for w in weights],
        out_specs=_rows(tm, D_MODEL),
        out_shape=jax.ShapeDtypeStruct((t, D_MODEL), F32),
        compiler_params=_params(1),
        name="mix",
    )(x, pm, g1, *os_, *ls_, pe, *weights)


def _t5_bucket(dist):
    max_exact = N_BUCKETS // 2
    d = np.asarray(dist)
    large = max_exact + (np.log(np.maximum(d, 1) / max_exact)
                         / np.log(MAX_DISTANCE / max_exact)
                         * (N_BUCKETS - max_exact)).astype(np.int64)
    large = np.minimum(large, N_BUCKETS - 1)
    return np.where(d < max_exact, d, large).astype(np.int32)


def _bias_by_key(rel_bias, g, win, dil):
    buckets = _t5_bucket(dil * np.arange(win // dil + 1))
    cols = rel_bias[:, g * HEADS_PER_GROUP:(g + 1) * HEADS_PER_GROUP].T
    edges = [0] + [j for j in range(1, len(buckets)) if buckets[j] != buckets[j - 1]] + [len(buckets)]
    runs = [jnp.broadcast_to(cols[:, buckets[a]:buckets[a] + 1], (HEADS_PER_GROUP, b - a))
            for a, b in zip(edges[:-1], edges[1:])]
    return jnp.concatenate(runs, axis=1).astype(F32)


def _prompt_bias(rel_bias, g, win, dil):
    nk = win // dil
    assert nk == Q_BLOCK
    neg = jnp.full((HEADS_PER_GROUP, Q_BLOCK - 1), NEG_INF, F32)
    diag = jnp.concatenate([neg, _bias_by_key(rel_bias, g, win, dil)[:, ::-1], neg, neg[:, :1]], axis=1)
    period = 3 * Q_BLOCK
    skew = jnp.tile(diag, (1, Q_BLOCK))[:, :Q_BLOCK * (period - 1)]
    return skew.reshape(HEADS_PER_GROUP, Q_BLOCK, period - 1)[:, :, Q_BLOCK - 1:]


def _sample_bias(rel_bias, g, win, dil, ts):
    by_key = _bias_by_key(rel_bias, g, win, dil)[:, ::-1]
    holes = jnp.full((HEADS_PER_GROUP, win // dil + 1, dil - 1), NEG_INF, F32)
    by_dist = jnp.concatenate([by_key[:, :, None], holes], axis=2).reshape(HEADS_PER_GROUP, -1)[:, :win + 1]
    rows = [jnp.pad(by_dist, ((0, 0), (t, KEY_PAD - 1 - t)), constant_values=NEG_INF) for t in range(ts)]
    return jnp.stack(rows, axis=1).reshape(HEADS_PER_GROUP * ts, win + KEY_PAD)


def _layer_weights(i, ts, norm_mix, w_in, conv_w, conv_b, conv_ln_g, conv_ln_b, w_a_out, q_norm, k_norm,
                   w_b_out, sgu_ln_g, sgu_ln_b, sgu_w, sgu_b, w_c_out, pool_w, pool_scale, w_d_out,
                   w_o, norm_ffn, w_gate, w_up, w_down, ple_norm, w_ple_gate, w_ple):
    row = lambda a: a[i].reshape(1, -1).astype(F32)
    bf = lambda a: a[i].astype(BF16)
    tril = np.tril(np.ones((CHUNK, CHUNK), dtype=bool))
    ws = jnp.where(tril[None], sgu_w[i], 0.0)
    gw = D_SGU // SGU_GROUPS
    pool_bd = jnp.zeros((D_POOL, D_POOL), F32)
    for g in range(len(POOL_WINDOWS)):
        pool_bd = lax.dynamic_update_slice(pool_bd, pool_w[i, g], (g * POOL_GW, g * POOL_GW))
    head_mean = np.kron(np.eye(N_HEADS), np.full((HEAD_DIM, HEAD_DIM), 1.0 / HEAD_DIM))
    return {
        "nmix": row(norm_mix), "w_in": bf(w_in), "conv_w": conv_w[i], "conv_b": row(conv_b),
        "cln_g": row(conv_ln_g), "cln_b": row(conv_ln_b), "w_a": bf(w_a_out),
        "qn": jnp.tile(q_norm[i], N_HEADS).reshape(1, D_ATTN),
        "kn": jnp.tile(k_norm[i], N_HEADS).reshape(1, D_ATTN),
        "head_mean": jnp.asarray(head_mean, BF16),
        "sln_g": row(sgu_ln_g), "sln_b": row(sgu_ln_b),
        "sgu_wcat": ws.reshape(SGU_GROUPS * CHUNK, CHUNK).astype(BF16),
        "sgu_c8": jnp.repeat(jnp.transpose(ws[:, :ts, :ts], (2, 1, 0)), gw, axis=2),
        "sgu_bias": jnp.repeat(sgu_b[i].T, gw, axis=1),
        "w_c": bf(w_c_out), "pool_wbd": pool_bd.astype(BF16), "pool_scale": row(pool_scale),
        "w_d": bf(w_d_out), "w_b": bf(w_b_out), "w_o": bf(w_o), "nffn": row(norm_ffn),
        "w_gate": bf(w_gate), "w_up": bf(w_up), "w_down": bf(w_down), "pnorm": row(ple_norm),
        "w_pg": bf(w_ple_gate), "w_p": bf(w_ple),
    }


def _kv_rows(k, v, g, keep):
    gs = slice(g * GROUP_W, (g + 1) * GROUP_W)
    t = k.shape[0]
    kk = k[t - keep:, gs].reshape(keep, HEADS_PER_GROUP, HEAD_DIM)
    vv = v[t - keep:, gs].reshape(keep, HEADS_PER_GROUP, HEAD_DIM)
    return jnp.stack([kk, vv], axis=1)


def kernel(x_prompt, x_sample, cache_kv_w128, cache_kv_w512, cache_kv_w2048, state_conv, state_pool, p_prompt, p_sample, rel_bias, norm_mix, w_in, conv_w, conv_b, conv_ln_g, conv_ln_b, w_a_out, q_norm, k_norm, w_b_out, sgu_ln_g, sgu_ln_b, sgu_w, sgu_b, w_c_out, pool_w, pool_scale, w_d_out, w_o, norm_ffn, w_gate, w_up, w_down, ple_norm, w_ple_gate, w_ple):
    bp, seq, _ = x_prompt.shape
    nb, ts, _ = x_sample.shape
    depth = w_in.shape[0]
    caches = (cache_kv_w128, cache_kv_w512, cache_kv_w2048)
    assert bp == 1 and seq % ATTN_TOKEN_BLOCK == 0
    assert ts % 8 == 0 and ts <= CHUNK
    for c, (win, _) in zip(caches, DIL_GROUPS):
        assert c.shape[2] == win, "sample caches must hold a full window"
    tm = 256

    prompt_bias = [_prompt_bias(rel_bias, g, win, dil) for g, (win, dil) in enumerate(DIL_GROUPS)]
    sample_bias = [_sample_bias(rel_bias, g, win, dil, ts) for g, (win, dil) in enumerate(DIL_GROUPS)]
    hmask = jnp.asarray(np.kron(np.eye(HEADS_PER_GROUP), np.ones((ts, HEAD_DIM))), F32)
    caches_t = [jnp.transpose(c, (0, 1, 3, 4, 5, 2)).reshape(depth, nb, 2, GROUP_W, c.shape[2])
                for c in caches]
    new_caches = None

    yp = x_prompt.reshape(seq, D_MODEL)
    ys = x_sample.reshape(nb * ts, D_MODEL)
    kvp = [[] for _ in DIL_GROUPS]
    conv_p, pool_p, conv_s, pool_s, sgu_s = [], [], [], [], []
    for i in range(depth):
        lw = _layer_weights(i, ts, norm_mix, w_in, conv_w, conv_b, conv_ln_g, conv_ln_b, w_a_out, q_norm,
                            k_norm, w_b_out, sgu_ln_g, sgu_ln_b, sgu_w, sgu_b, w_c_out, pool_w, pool_scale,
                            w_d_out, w_o, norm_ffn, w_gate, w_up, w_down, ple_norm, w_ple_gate, w_ple)

        pm, g1, q, k, v, cst, pst = _proj_prompt(yp, lw, tm)
        os_, ls_ = [], []
        for g, (win, dil) in enumerate(DIL_GROUPS):
            o, lse = _attn_prompt(q, k, v, prompt_bias[g], g, dil)
            os_.append(o)
            ls_.append(lse)
            kvp[g].append(_kv_rows(k, v, g, min(win, seq))[None])
        yp = _mix(yp, pm, g1, os_, ls_, p_prompt[i, 0], lw, tm)
        conv_p.append(cst[CONV_PAD - CONV_STATE:][None])
        pool_p.append(pst[POOL_PAD - POOL_STATE:][None])

        sconv = jnp.pad(state_conv[i], ((0, 0), (CONV_PAD - CONV_STATE, 0), (0, 0)))
        spool = jnp.pad(state_pool[i], ((0, 0), (POOL_PAD - POOL_STATE, 0), (0, 0)))
        pm, g1, q, k, v, cst, pst, vv = _proj_sample(ys, sconv, spool, lw, nb, ts, PAST_LEN)
        os_, ls_, new_caches = _attn_sample(i, q, k, v, caches_t, new_caches, sample_bias, hmask, nb, ts)
        ys = _mix(ys, pm, g1, os_, ls_, p_sample[i].reshape(nb * ts, D_PLE), lw, nb * ts)
        conv_s.append(cst[:, CONV_PAD - CONV_STATE:])
        pool_s.append(pst[:, POOL_PAD - POOL_STATE:])
        sgu_s.append(vv.reshape(nb, ts, D_SGU))

    kvs = [jnp.transpose(c.reshape(depth, nb, 2, HEADS_PER_GROUP, HEAD_DIM, c.shape[4]), (0, 1, 5, 2, 3, 4))
           for c in new_caches]
    return (yp.reshape(bp, seq, D_MODEL), ys.reshape(nb, ts, D_MODEL),
            jnp.stack(kvp[0]), jnp.stack(kvp[1]), jnp.stack(kvp[2]),
            jnp.stack(conv_p), jnp.stack(pool_p),
            kvs[0], kvs[1], kvs[2],
            jnp.stack(conv_s), jnp.stack(pool_s), jnp.stack(sgu_s))
```

```python
import functools

import numpy as np
import jax
import jax.numpy as jnp
from jax import lax
from jax.experimental import pallas as pl
from jax.experimental.pallas import tpu as pltpu

F32 = jnp.float32
BF16 = jnp.bfloat16

D_MODEL = 1024
PAST_LEN = 16384
D_CONV = 384
CONV_WIDTH = 31
CONV_STATE = CONV_WIDTH - 1
HEAD_DIM = 64
DIL_GROUPS = ((128, 1), (512, 4), (2048, 16))
HEADS_PER_GROUP = 4
N_HEADS = HEADS_PER_GROUP * len(DIL_GROUPS)
D_ATTN = N_HEADS * HEAD_DIM
GROUP_W = HEADS_PER_GROUP * HEAD_DIM
N_BUCKETS = 32
MAX_DISTANCE = 2048
D_SGU = 384
SGU_GROUPS = 4
CHUNK = 128
D_POOL = 384
POOL_WINDOWS = (2, 4, 8, 16)
POOL_STATE = max(POOL_WINDOWS) - 1
POOL_GW = D_POOL // len(POOL_WINDOWS)
D_FF = 2816
D_PLE = 256
N_BRANCH = 4
COL_A = 0
COL_Q = 2 * D_CONV
COL_K = COL_Q + D_ATTN
COL_V = COL_K + D_ATTN
COL_C = COL_V + D_ATTN
COL_D = COL_C + 2 * D_SGU
COL_G = COL_D + D_POOL
D_IN = COL_G + N_BRANCH * D_MODEL
EPS = 1e-6
LN_EPS = 1e-5
NEG_INF = -1e30

Q_BLOCK = 128
ATTN_TOKEN_BLOCK = 2048
CONV_PAD = 32
POOL_PAD = 16
KEY_PAD = 128
ROW_BLOCK = 32
V7X_VMEM_LIMIT_BYTES = 56 * 1024 * 1024


def _dot(a, b):
    return jnp.dot(a.astype(BF16), b.astype(BF16), preferred_element_type=F32)


def _dot_nt(a, b):
    return lax.dot_general(a.astype(BF16), b.astype(BF16), (((1,), (1,)), ((), ())),
                           preferred_element_type=F32)


def _rmsnorm(x, g):
    return x * lax.rsqrt(jnp.mean(x * x, axis=-1, keepdims=True) + EPS) * g


def _layernorm(x, g, b):
    mu = jnp.mean(x, axis=-1, keepdims=True)
    xc = x - mu
    return xc * lax.rsqrt(jnp.mean(xc * xc, axis=-1, keepdims=True) + LN_EPS) * g + b


def _sigmoid(x):
    return 0.5 * jnp.tanh(0.5 * x) + 0.5


def _silu(x):
    return x * _sigmoid(x)


def _head_rmsnorm(z, head_mean_ref, g):
    ms = _dot(z * z, head_mean_ref[...])
    return z * lax.rsqrt(ms + EPS) * g


def _by_channel_group(vals, n_channels):
    gw = n_channels // len(vals)
    ch = lax.broadcasted_iota(jnp.int32, (1, n_channels), 1)
    out = vals[-1]
    for g in range(len(vals) - 2, -1, -1):
        out = jnp.where(ch < (g + 1) * gw, vals[g], out)
    return out


def _proj_common(x_ref, nmix_ref, win_ref):
    h = _rmsnorm(x_ref[...], nmix_ref[...]).astype(BF16)

    def proj(lo, hi):
        return jnp.dot(h, win_ref[:, lo:hi], preferred_element_type=F32)
    return proj


def _qkv_and_gate1(proj, qn_ref, kn_ref, hm_ref, q_ref, k_ref, v_ref, g1_ref):
    q_ref[...] = _head_rmsnorm(proj(COL_Q, COL_K), hm_ref, qn_ref[...]) * (HEAD_DIM ** -0.5)
    k_ref[...] = _head_rmsnorm(proj(COL_K, COL_V), hm_ref, kn_ref[...])
    v_ref[...] = proj(COL_V, COL_C)
    g1_ref[...] = _sigmoid(proj(COL_G + D_MODEL, COL_G + 2 * D_MODEL))


def _gate(proj, branch):
    lo = COL_G + branch * D_MODEL
    return _sigmoid(proj(lo, lo + D_MODEL))


def _proj_prompt_kernel(x_ref, nmix_ref, win_ref, convw_ref, convb_ref, clng_ref, clnb_ref, wa_ref,
                        qn_ref, kn_ref, hm_ref, slng_ref, slnb_ref, swcat_ref, sbias_ref, wc_ref,
                        pwbd_ref, pscale_ref, wd_ref,
                        pm_ref, g1_ref, q_ref, k_ref, v_ref, cst_ref, pst_ref,
                        cbuf, pbuf, *, tm):
    i = pl.program_id(0)

    @pl.when(i == 0)
    def _():
        cbuf[0:CONV_PAD, :] = jnp.zeros((CONV_PAD, D_CONV), F32)
        pbuf[0:POOL_PAD, :] = jnp.zeros((POOL_PAD, D_POOL), F32)

    proj = _proj_common(x_ref, nmix_ref, win_ref)

    za = proj(COL_A, COL_Q)
    cbuf[CONV_PAD:CONV_PAD + tm, :] = za[:, :D_CONV] * _sigmoid(za[:, D_CONV:])
    zc = proj(COL_C, COL_D)
    zd = proj(COL_D, COL_G)

    gates = {}

    def emit_q():
        q_ref[...] = _head_rmsnorm(proj(COL_Q, COL_K), hm_ref, qn_ref[...]) * (HEAD_DIM ** -0.5)

    def emit_k():
        k_ref[...] = _head_rmsnorm(proj(COL_K, COL_V), hm_ref, kn_ref[...])

    def emit_v():
        v_ref[...] = proj(COL_V, COL_C)

    def emit_g1():
        g1_ref[...] = _gate(proj, 1)

    def emit_gate(branch):
        gates[branch] = _gate(proj, branch)

    fillers = [emit_q, emit_k, emit_v, emit_g1, functools.partial(emit_gate, 0),
               functools.partial(emit_gate, 2), functools.partial(emit_gate, 3)]

    cvs = []
    for blk in range(tm // ROW_BLOCK):
        base = blk * ROW_BLOCK
        acc = jnp.broadcast_to(convb_ref[...], (ROW_BLOCK, D_CONV))
        for shift in range(8):
            offs = [j + CONV_PAD - CONV_STATE for j in range(CONV_WIDTH)
                    if (j + CONV_PAD - CONV_STATE) % 8 == shift]
            rows = ROW_BLOCK if shift == 0 else ROW_BLOCK + 8
            part = None
            for off in offs:
                j = off - (CONV_PAD - CONV_STATE)
                term = convw_ref[j:j + 1, :] * cbuf[pl.ds(base + off - shift, rows), :]
                part = term if part is None else part + term
            acc = acc + part[shift:shift + ROW_BLOCK, :]
        cvs.append(_silu(_layernorm(acc, clng_ref[...], clnb_ref[...])))
        if blk < len(fillers):
            fillers[blk]()
    for filler in fillers[tm // ROW_BLOCK:]:
        filler()
    pm = gates[0] * _dot(jnp.concatenate(cvs, axis=0), wa_ref[...])
    tail = cbuf[tm:tm + CONV_PAD, :]
    cst_ref[...] = tail
    cbuf[0:CONV_PAD, :] = tail

    u = zc[:, :D_SGU]
    vv = _layernorm(zc[:, D_SGU:], slng_ref[...], slnb_ref[...])
    svs = []
    for c in range(tm // CHUNK):
        r = _dot(swcat_ref[...], vv[c * CHUNK:(c + 1) * CHUNK, :])
        sv = _by_channel_group([r[g * CHUNK:(g + 1) * CHUNK, :] for g in range(SGU_GROUPS)], D_SGU)
        svs.append(sv + sbias_ref[...])
    sv = jnp.concatenate(svs, axis=0) if len(svs) > 1 else svs[0]
    pm = pm + gates[2] * _dot(u * sv, wc_ref[...])

    pbuf[POOL_PAD:POOL_PAD + tm, :] = zd
    pooled = []
    for blk in range(tm // ROW_BLOCK):
        base = blk * ROW_BLOCK
        pos1 = i * tm + base + lax.broadcasted_iota(jnp.int32, (ROW_BLOCK, 1), 0) + 1
        run = pbuf[base:base + POOL_PAD + ROW_BLOCK, :]
        cur = run[POOL_PAD:, :]
        width = 1
        sums, cnts = [], []
        for w in POOL_WINDOWS:
            while width < w:
                run = run + pltpu.roll(run, width, 0)
                width *= 2
            sums.append(run[POOL_PAD:, :])
            cnts.append(jnp.minimum(pos1, w).astype(F32))
        pooled.append(_by_channel_group(sums, D_POOL) / _by_channel_group(cnts, D_POOL) - cur)
    pooled = jnp.concatenate(pooled, axis=0)
    mixed = _dot(pooled, pwbd_ref[...]) * pscale_ref[...]
    pm = pm + gates[3] * _dot(mixed, wd_ref[...])
    ptail = pbuf[tm:tm + POOL_PAD, :]
    pst_ref[...] = ptail
    pbuf[0:POOL_PAD, :] = ptail

    pm_ref[...] = pm


def _proj_sample_kernel(x_ref, sconv_ref, spool_ref, nmix_ref, win_ref, convw_ref, convb_ref,
                        clng_ref, clnb_ref, wa_ref, qn_ref, kn_ref, hm_ref, slng_ref, slnb_ref,
                        csgu_ref, sbias_ref, wc_ref, pwbd_ref, pscale_ref, wd_ref,
                        pm_ref, g1_ref, q_ref, k_ref, v_ref, cst_ref, pst_ref, vv_ref,
                        cbuf, pbuf, *, nb, ts, pos0):
    n = nb * ts
    proj = _proj_common(x_ref, nmix_ref, win_ref)

    za = proj(COL_A, COL_Q)
    glu = za[:, :D_CONV] * _sigmoid(za[:, D_CONV:])
    cbuf[:, 0:CONV_PAD, :] = sconv_ref[...]
    cbuf[:, CONV_PAD:CONV_PAD + ts, :] = glu.reshape(nb, ts, D_CONV)
    acc = jnp.broadcast_to(convb_ref[...].reshape(1, 1, D_CONV), (nb, ts, D_CONV))
    for j in range(CONV_WIDTH):
        w = convw_ref[j:j + 1, :].reshape(1, 1, D_CONV)
        acc = acc + w * cbuf[:, pl.ds(j + CONV_PAD - CONV_STATE, ts), :]
    cv = _silu(_layernorm(acc.reshape(n, D_CONV), clng_ref[...], clnb_ref[...]))
    pm = _gate(proj, 0) * _dot(cv, wa_ref[...])
    cst_ref[...] = cbuf[:, ts:ts + CONV_PAD, :]

    zc = proj(COL_C, COL_D)
    u = zc[:, :D_SGU]
    vv = _layernorm(zc[:, D_SGU:], slng_ref[...], slnb_ref[...])
    vv_ref[...] = vv
    vv3 = vv.reshape(nb, ts, D_SGU)
    sv = jnp.broadcast_to(sbias_ref[...].reshape(1, ts, D_SGU), (nb, ts, D_SGU))
    for j in range(ts):
        sv = sv + csgu_ref[j].reshape(1, ts, D_SGU) * vv3[:, j:j + 1, :]
    pm = pm + _gate(proj, 2) * _dot(u * sv.reshape(n, D_SGU), wc_ref[...])

    zd = proj(COL_D, COL_G)
    pbuf[:, 0:POOL_PAD, :] = spool_ref[...]
    pbuf[:, POOL_PAD:POOL_PAD + ts, :] = zd.reshape(nb, ts, D_POOL)
    pos1 = pos0 + lax.broadcasted_iota(jnp.int32, (1, ts, 1), 1) + 1
    run = jnp.zeros((nb, ts, D_POOL), F32)
    sums, cnts = [], []
    for s in range(max(POOL_WINDOWS)):
        run = run + pbuf[:, pl.ds(POOL_PAD - s, ts), :]
        if s + 1 in POOL_WINDOWS:
            sums.append(run)
            cnts.append(jnp.minimum(pos1, s + 1).astype(F32))
    ch = lax.broadcasted_iota(jnp.int32, (1, 1, D_POOL), 2)
    win_sum, cnt = sums[-1], cnts[-1]
    for g in range(len(POOL_WINDOWS) - 2, -1, -1):
        win_sum = jnp.where(ch < (g + 1) * POOL_GW, sums[g], win_sum)
        cnt = jnp.where(ch < (g + 1) * POOL_GW, cnts[g], cnt)
    pooled = (win_sum / cnt).reshape(n, D_POOL) - zd
    mixed = _dot(pooled, pwbd_ref[...]) * pscale_ref[...]
    pm = pm + _gate(proj, 3) * _dot(mixed, wd_ref[...])
    pst_ref[...] = pbuf[:, ts:ts + POOL_PAD, :]

    pm_ref[...] = pm
    _qkv_and_gate1(proj, qn_ref, kn_ref, hm_ref, q_ref, k_ref, v_ref, g1_ref)


def _split_classes(src_ref, dil, tmp, dst, dst_off):
    n = src_ref.shape[0]
    if dil == 1:
        dst[0, dst_off:dst_off + n, :] = src_ref[...]
    elif dil == 4:
        for r in range(4):
            dst[r, dst_off:dst_off + n // 4, :] = src_ref[pl.ds(r, n // 4, stride=4), :]
    else:
        assert dil == 16
        for c in range(4):
            tmp[c, 0:n // 4, :] = src_ref[pl.ds(c, n // 4, stride=4), :]
        for c in range(4):
            for d in range(4):
                dst[c + 4 * d, dst_off:dst_off + n // 16, :] = tmp[c, pl.ds(d, n // 16, stride=4), :]


def _merge_classes(dst_ref, dil, tmp, src):
    n = dst_ref.shape[0]
    if dil == 1:
        dst_ref[...] = src[0]
    elif dil == 4:
        for r in range(4):
            dst_ref[pl.ds(r, n // 4, stride=4), :] = src[r]
    else:
        assert dil == 16
        for c in range(4):
            for d in range(4):
                tmp[c, pl.ds(d, n // 16, stride=4), :] = src[c + 4 * d]
        for c in range(4):
            dst_ref[pl.ds(c, n // 4, stride=4), :] = tmp[c, 0:n // 4, :]


def _attn_prompt_kernel(q_ref, hk_ref, kc_ref, hv_ref, vc_ref, bias_ref, o_ref, l_ref,
                        qs, ks, vs, s_scr, p_scr, m_scr, os_, ls_, b0_scr, tmp, *, dil):
    tb = q_ref.shape[0]
    n_sub = tb // (dil * Q_BLOCK)
    units = [(r, j) for r in range(dil) for j in range(n_sub)]
    first = pl.program_id(0) == 0

    _split_classes(q_ref, dil, tmp, qs, 0)
    _split_classes(hk_ref, dil, tmp, ks, 0)
    _split_classes(kc_ref, dil, tmp, ks, Q_BLOCK)
    _split_classes(hv_ref, dil, tmp, vs, 0)
    _split_classes(vc_ref, dil, tmp, vs, Q_BLOCK)

    col = lax.broadcasted_iota(jnp.int32, (2 * Q_BLOCK, 2 * Q_BLOCK), 1)
    b0_scr[...] = jnp.where(jnp.logical_and(col < Q_BLOCK, first), NEG_INF, bias_ref[0])

    lane = lax.broadcasted_iota(jnp.int32, (1, 128), 1)
    head0 = lane < HEAD_DIM
    for u, (r, j) in enumerate(units):
        q = qs[r, j * Q_BLOCK:(j + 1) * Q_BLOCK, :]
        q2 = jnp.concatenate([jnp.where(head0, q, 0.0), jnp.where(head0, 0.0, q)], axis=0)
        bias = b0_scr[...] if j == 0 else bias_ref[0]
        s_scr[u] = _dot_nt(q2, ks[r, j * Q_BLOCK:(j + 2) * Q_BLOCK, :]) + bias

    for u in range(len(units)):
        s = s_scr[u]
        m = jnp.max(s, axis=-1, keepdims=True)
        p_scr[u] = jnp.exp(s - m).astype(BF16)
        m_scr[u] = jnp.broadcast_to(m, (2 * Q_BLOCK, 128))

    ones = jnp.ones((2 * Q_BLOCK, 128), BF16)
    for u, (r, j) in enumerate(units):
        v = vs[r, j * Q_BLOCK:(j + 2) * Q_BLOCK, :].astype(BF16)
        o2 = jnp.dot(p_scr[u], jnp.concatenate([v, ones], axis=1), preferred_element_type=F32)
        num = jnp.where(head0, o2[:Q_BLOCK, :128], o2[Q_BLOCK:, :128])
        den = jnp.where(head0, o2[:Q_BLOCK, 128:], o2[Q_BLOCK:, 128:])
        mx = jnp.where(head0, m_scr[u, :Q_BLOCK, :], m_scr[u, Q_BLOCK:, :])
        os_[r, j * Q_BLOCK:(j + 1) * Q_BLOCK, :] = num / den
        ls_[r, j * Q_BLOCK:(j + 1) * Q_BLOCK, :] = mx + jnp.log(den)

    _merge_classes(o_ref, dil, tmp, os_)
    _merge_classes(l_ref, dil, tmp, ls_)


def _attn_sample_kernel(*refs, ts, aliased):
    n_g = len(DIL_GROUPS)
    q_ref, k_ref, v_ref = refs[0:3]
    caches = refs[3:3 + n_g]
    biases = refs[3 + n_g:3 + 2 * n_g]
    hmask_ref = refs[3 + 2 * n_g]
    pos = 4 + 2 * n_g + (n_g if aliased else 0)
    outs = refs[pos:pos + n_g]
    lses = refs[pos + n_g:pos + 2 * n_g]
    news = refs[pos + 2 * n_g:pos + 3 * n_g]
    exts = refs[pos + 3 * n_g:pos + 4 * n_g]

    for g in range(n_g):
        ext = exts[g]
        length = ext.shape[2] - KEY_PAD
        gs = slice(g * GROUP_W, (g + 1) * GROUP_W)
        zeros = jnp.zeros((KEY_PAD - ts, GROUP_W), F32)
        for kv, new_ref in enumerate((k_ref, v_ref)):
            ext[kv, :, 0:length] = caches[g][0, 0, kv]
            ext[kv, :, length:length + KEY_PAD] = jnp.concatenate([new_ref[:, gs], zeros], axis=0).T
            news[g][0, 0, kv] = ext[kv][:, ts:ts + length]
        for later in range(1, news[g].shape[0]):
            news[g][later, 0] = jnp.zeros(news[g].shape[2:], F32)

        qm = jnp.concatenate([q_ref[:, gs]] * HEADS_PER_GROUP, axis=0) * hmask_ref[...]
        s = _dot(qm, ext[0]) + biases[g][...]
        m = jnp.max(s, axis=-1, keepdims=True)
        p = jnp.exp(s - m)
        den = jnp.sum(p, axis=-1, keepdims=True)
        o = _dot_nt(p, ext[1]) / den
        lse = m + jnp.log(den)
        for h in range(HEADS_PER_GROUP):
            sl = slice(h * HEAD_DIM, (h + 1) * HEAD_DIM)
            rows = slice(h * ts, (h + 1) * ts)
            outs[g][:, sl] = o[rows, sl]
            lses[g][:, sl] = jnp.broadcast_to(lse[rows, :], (ts, HEAD_DIM))


def _mix_kernel(x_ref, pm_ref, g1_ref, o1_ref, o2_ref, o3_ref, l1_ref, l2_ref, l3_ref, pe_ref,
                wb_ref, wo_ref, nffn_ref, wg_ref, wu_ref, wdn_ref, pnorm_ref, wpg_ref, wp_ref, y_ref):
    l1, l2, l3 = l1_ref[...], l2_ref[...], l3_ref[...]
    mx = jnp.maximum(jnp.maximum(l1, l2), l3)
    e1, e2, e3 = jnp.exp(l1 - mx), jnp.exp(l2 - mx), jnp.exp(l3 - mx)
    o = (e1 * o1_ref[...] + e2 * o2_ref[...] + e3 * o3_ref[...]) / (e1 + e2 + e3)
    merged = pm_ref[...] + g1_ref[...] * _dot(o, wb_ref[...])
    x = x_ref[...] + _dot(merged, wo_ref[...])
    h2 = _rmsnorm(x, nffn_ref[...]).astype(BF16)
    ff = _silu(_dot(h2, wg_ref[...])) * _dot(h2, wu_ref[...])
    x = x + _dot(ff, wdn_ref[...])
    h3 = _rmsnorm(x, pnorm_ref[...])
    y_ref[...] = x + _dot(pe_ref[...], wp_ref[...]) * _sigmoid(_dot(h3, wpg_ref[...]))


def _resident(shape):
    nd = len(shape)
    return pl.BlockSpec(shape, lambda *_: (0,) * nd, pipeline_mode=pl.Buffered(1))


def _rows(tm, width):
    return pl.BlockSpec((tm, width), lambda i: (i, 0))


def _params(n_axes, flags=None):
    return pltpu.CompilerParams(dimension_semantics=("arbitrary",) * n_axes,
                                vmem_limit_bytes=V7X_VMEM_LIMIT_BYTES, flags=flags)


def _proj_weight_args(lw):
    return (lw["nmix"], lw["w_in"], lw["conv_w"], lw["conv_b"], lw["cln_g"], lw["cln_b"], lw["w_a"],
            lw["qn"], lw["kn"], lw["head_mean"], lw["sln_g"], lw["sln_b"])


def _proj_prompt(x, lw, tm):
    t = x.shape[0]
    mid = (lw["sgu_wcat"], lw["sgu_bias"], lw["w_c"], lw["pool_wbd"], lw["pool_scale"], lw["w_d"])
    weights = _proj_weight_args(lw) + mid
    f = lambda w: jax.ShapeDtypeStruct((t, w), F32)
    return pl.pallas_call(
        functools.partial(_proj_prompt_kernel, tm=tm),
        grid=(t // tm,),
        in_specs=[_rows(tm, D_MODEL)] + [_resident(w.shape) for w in weights],
        out_specs=[_rows(tm, D_MODEL), _rows(tm, D_MODEL), _rows(tm, D_ATTN), _rows(tm, D_ATTN),
                   _rows(tm, D_ATTN),
                   pl.BlockSpec((CONV_PAD, D_CONV), lambda i: (0, 0)),
                   pl.BlockSpec((POOL_PAD, D_POOL), lambda i: (0, 0))],
        out_shape=[f(D_MODEL), f(D_MODEL), f(D_ATTN), f(D_ATTN), f(D_ATTN),
                   jax.ShapeDtypeStruct((CONV_PAD, D_CONV), F32),
                   jax.ShapeDtypeStruct((POOL_PAD, D_POOL), F32)],
        scratch_shapes=[pltpu.VMEM((tm + CONV_PAD, D_CONV), F32),
                        pltpu.VMEM((tm + POOL_PAD, D_POOL), F32)],
        compiler_params=_params(1),
        name="proj_prompt",
    )(x, *weights)


def _proj_sample(x, sconv, spool, lw, nb, ts, pos0):
    n = nb * ts
    mid = (lw["sgu_c8"], lw["sgu_bias"][:ts], lw["w_c"], lw["pool_wbd"], lw["pool_scale"], lw["w_d"])
    args = (x, sconv, spool) + _proj_weight_args(lw) + mid
    f = lambda w: jax.ShapeDtypeStruct((n, w), F32)
    full = lambda a: pl.BlockSpec(a.shape, lambda i, nd=a.ndim: (0,) * nd)
    return pl.pallas_call(
        functools.partial(_proj_sample_kernel, nb=nb, ts=ts, pos0=pos0),
        grid=(1,),
        in_specs=[full(a) for a in args],
        out_specs=[pl.BlockSpec((n, D_MODEL), lambda i: (0, 0)), pl.BlockSpec((n, D_MODEL), lambda i: (0, 0)),
                   pl.BlockSpec((n, D_ATTN), lambda i: (0, 0)), pl.BlockSpec((n, D_ATTN), lambda i: (0, 0)),
                   pl.BlockSpec((n, D_ATTN), lambda i: (0, 0)),
                   pl.BlockSpec((nb, CONV_PAD, D_CONV), lambda i: (0, 0, 0)),
                   pl.BlockSpec((nb, POOL_PAD, D_POOL), lambda i: (0, 0, 0)),
                   pl.BlockSpec((n, D_SGU), lambda i: (0, 0))],
        out_shape=[f(D_MODEL), f(D_MODEL), f(D_ATTN), f(D_ATTN), f(D_ATTN),
                   jax.ShapeDtypeStruct((nb, CONV_PAD, D_CONV), F32),
                   jax.ShapeDtypeStruct((nb, POOL_PAD, D_POOL), F32),
                   f(D_SGU)],
        scratch_shapes=[pltpu.VMEM((nb, CONV_PAD + ts, D_CONV), F32),
                        pltpu.VMEM((nb, POOL_PAD + ts, D_POOL), F32)],
        compiler_params=_params(1),
        name="proj_sample",
    )(*args)


def _attn_prompt(q, k, v, bias, g, dil):
    t = q.shape[0]
    tb = ATTN_TOKEN_BLOCK
    hb = dil * Q_BLOCK
    lane_blocks = GROUP_W // 128
    cur = pl.BlockSpec((tb, 128), lambda b, hp: (b, lane_blocks * g + hp))
    halo = pl.BlockSpec((hb, 128), lambda b, hp: (jnp.maximum(b * (tb // hb) - 1, 0), lane_blocks * g + hp))
    out = pl.BlockSpec((tb, 128), lambda b, hp: (b, hp))
    n_sub = tb // hb
    n_units = dil * n_sub
    rows2 = 2 * Q_BLOCK
    o, lse = pl.pallas_call(
        functools.partial(_attn_prompt_kernel, dil=dil),
        grid=(t // tb, lane_blocks),
        in_specs=[cur, halo, cur, halo, cur,
                  pl.BlockSpec((1, rows2, 2 * Q_BLOCK), lambda b, hp: (hp, 0, 0))],
        out_specs=[out, out],
        out_shape=[jax.ShapeDtypeStruct((t, GROUP_W), F32)] * 2,
        scratch_shapes=[pltpu.VMEM((dil, n_sub * Q_BLOCK, 128), F32),
                        pltpu.VMEM((dil, (n_sub + 1) * Q_BLOCK, 128), F32),
                        pltpu.VMEM((dil, (n_sub + 1) * Q_BLOCK, 128), F32),
                        pltpu.VMEM((n_units, rows2, 2 * Q_BLOCK), F32),
                        pltpu.VMEM((n_units, rows2, 2 * Q_BLOCK), BF16),
                        pltpu.VMEM((n_units, rows2, 128), F32),
                        pltpu.VMEM((dil, n_sub * Q_BLOCK, 128), F32),
                        pltpu.VMEM((dil, n_sub * Q_BLOCK, 128), F32),
                        pltpu.VMEM((rows2, 2 * Q_BLOCK), F32),
                        pltpu.VMEM((4, tb // 4, 128), F32)],
        compiler_params=_params(2),
        name=f"attn_prompt_d{dil}",
    )(q, k, k, v, v, bias.reshape(lane_blocks, rows2, 2 * Q_BLOCK))
    return o, lse


def _attn_sample(layer, q, k, v, caches_t, new_prev, biases, hmask, nb, ts):
    n = nb * ts
    aliased = new_prev is not None
    tok = lambda w: pl.BlockSpec((ts, w), lambda b: (b, 0))
    cache_spec = lambda c: pl.BlockSpec((1, 1) + c.shape[2:], lambda b: (layer, b, 0, 0, 0))
    if aliased:
        new_spec = cache_spec
    else:
        assert layer == 0
        new_spec = lambda c: pl.BlockSpec((c.shape[0], 1) + c.shape[2:], lambda b: (0, b, 0, 0, 0))
    const = lambda a: pl.BlockSpec(a.shape, lambda b: (0, 0))
    n_in = 3 + 2 * len(caches_t) + 1
    res = pl.pallas_call(
        functools.partial(_attn_sample_kernel, ts=ts, aliased=aliased),
        grid=(nb,),
        in_specs=[tok(D_ATTN)] * 3 + [cache_spec(c) for c in caches_t] + [const(b) for b in biases]
                 + [const(hmask)]
                 + ([pl.BlockSpec(memory_space=pl.ANY)] * len(caches_t) if aliased else []),
        out_specs=[tok(GROUP_W)] * 6 + [new_spec(c) for c in caches_t],
        out_shape=[jax.ShapeDtypeStruct((n, GROUP_W), F32)] * 6
                  + [jax.ShapeDtypeStruct(c.shape, F32) for c in caches_t],
        scratch_shapes=[pltpu.VMEM((2, GROUP_W, c.shape[4] + KEY_PAD), F32) for c in caches_t],
        input_output_aliases={n_in + i: 6 + i for i in range(len(caches_t))} if aliased else {},
        compiler_params=_params(1),
        name="attn_sample",
    )(q, k, v, *caches_t, *biases, hmask, *(new_prev if aliased else ()))
    return res[0:3], res[3:6], res[6:9]


def _mix(x, pm, g1, os_, ls_, pe, lw, tm):
    t = x.shape[0]
    weights = (lw["w_b"], lw["w_o"], lw["nffn"], lw["w_gate"], lw["w_up"], lw["w_down"],
               lw["pnorm"], lw["w_pg"], lw["w_p"])
    return pl.pallas_call(
        _mix_kernel,
        grid=(t // tm,),
        in_specs=[_rows(tm, D_MODEL)] * 3 + [_rows(tm, GROUP_W)] * 6 + [_rows(tm, D_PLE)]
                 + [_resident(w.shape) for w in weights],
        out_specs=_rows(tm, D_MODEL),
        out_shape=jax.ShapeDtypeStruct((t, D_MODEL), F32),
        compiler_params=_params(1),
        name="mix",
    )(x, pm, g1, *os_, *ls_, pe, *weights)


def _t5_bucket(dist):
    max_exact = N_BUCKETS // 2
    d = np.asarray(dist)
    large = max_exact + (np.log(np.maximum(d, 1) / max_exact)
                         / np.log(MAX_DISTANCE / max_exact)
                         * (N_BUCKETS - max_exact)).astype(np.int64)
    large = np.minimum(large, N_BUCKETS - 1)
    return np.where(d < max_exact, d, large).astype(np.int32)


def _bias_by_key(rel_bias, g, win, dil):
    buckets = _t5_bucket(dil * np.arange(win // dil + 1))
    cols = rel_bias[:, g * HEADS_PER_GROUP:(g + 1) * HEADS_PER_GROUP].T
    edges = [0] + [j for j in range(1, len(buckets)) if buckets[j] != buckets[j - 1]] + [len(buckets)]
    runs = [jnp.broadcast_to(cols[:, buckets[a]:buckets[a] + 1], (HEADS_PER_GROUP, b - a))
            for a, b in zip(edges[:-1], edges[1:])]
    return jnp.concatenate(runs, axis=1).astype(F32)


def _prompt_bias(rel_bias, g, win, dil):
    nk = win // dil
    assert nk == Q_BLOCK
    neg = jnp.full((HEADS_PER_GROUP, Q_BLOCK - 1), NEG_INF, F32)
    diag = jnp.concatenate([neg, _bias_by_key(rel_bias, g, win, dil)[:, ::-1], neg, neg[:, :1]], axis=1)
    period = 3 * Q_BLOCK
    skew = jnp.tile(diag, (1, Q_BLOCK))[:, :Q_BLOCK * (period - 1)]
    return skew.reshape(HEADS_PER_GROUP, Q_BLOCK, period - 1)[:, :, Q_BLOCK - 1:]


def _sample_bias(rel_bias, g, win, dil, ts):
    by_key = _bias_by_key(rel_bias, g, win, dil)[:, ::-1]
    holes = jnp.full((HEADS_PER_GROUP, win // dil + 1, dil - 1), NEG_INF, F32)
    by_dist = jnp.concatenate([by_key[:, :, None], holes], axis=2).reshape(HEADS_PER_GROUP, -1)[:, :win + 1]
    rows = [jnp.pad(by_dist, ((0, 0), (t, KEY_PAD - 1 - t)), constant_values=NEG_INF) for t in range(ts)]
    return jnp.stack(rows, axis=1).reshape(HEADS_PER_GROUP * ts, win + KEY_PAD)


def _layer_weights(i, ts, norm_mix, w_in, conv_w, conv_b, conv_ln_g, conv_ln_b, w_a_out, q_norm, k_norm,
                   w_b_out, sgu_ln_g, sgu_ln_b, sgu_w, sgu_b, w_c_out, pool_w, pool_scale, w_d_out,
                   w_o, norm_ffn, w_gate, w_up, w_down, ple_norm, w_ple_gate, w_ple):
    row = lambda a: a[i].reshape(1, -1).astype(F32)
    bf = lambda a: a[i].astype(BF16)
    tril = np.tril(np.ones((CHUNK, CHUNK), dtype=bool))
    ws = jnp.where(tril[None], sgu_w[i], 0.0)
    gw = D_SGU // SGU_GROUPS
    pool_bd = jnp.zeros((D_POOL, D_POOL), F32)
    for g in range(len(POOL_WINDOWS)):
        pool_bd = lax.dynamic_update_slice(pool_bd, pool_w[i, g], (g * POOL_GW, g * POOL_GW))
    head_mean = np.kron(np.eye(N_HEADS), np.full((HEAD_DIM, HEAD_DIM), 1.0 / HEAD_DIM))
    return {
        "nmix": row(norm_mix), "w_in": bf(w_in), "conv_w": conv_w[i], "conv_b": row(conv_b),
        "cln_g": row(conv_ln_g), "cln_b": row(conv_ln_b), "w_a": bf(w_a_out),
        "qn": jnp.tile(q_norm[i], N_HEADS).reshape(1, D_ATTN),
        "kn": jnp.tile(k_norm[i], N_HEADS).reshape(1, D_ATTN),
        "head_mean": jnp.asarray(head_mean, BF16),
        "sln_g": row(sgu_ln_g), "sln_b": row(sgu_ln_b),
        "sgu_wcat": ws.reshape(SGU_GROUPS * CHUNK, CHUNK).astype(BF16),
        "sgu_c8": jnp.repeat(jnp.transpose(ws[:, :ts, :ts], (2, 1, 0)), gw, axis=2),
        "sgu_bias": jnp.repeat(sgu_b[i].T, gw, axis=1),
        "w_c": bf(w_c_out), "pool_wbd": pool_bd.astype(BF16), "pool_scale": row(pool_scale),
        "w_d": bf(w_d_out), "w_b": bf(w_b_out), "w_o": bf(w_o), "nffn": row(norm_ffn),
        "w_gate": bf(w_gate), "w_up": bf(w_up), "w_down": bf(w_down), "pnorm": row(ple_norm),
        "w_pg": bf(w_ple_gate), "w_p": bf(w_ple),
    }


def _kv_rows(k, v, g, keep):
    gs = slice(g * GROUP_W, (g + 1) * GROUP_W)
    t = k.shape[0]
    kk = k[t - keep:, gs].reshape(keep, HEADS_PER_GROUP, HEAD_DIM)
    vv = v[t - keep:, gs].reshape(keep, HEADS_PER_GROUP, HEAD_DIM)
    return jnp.stack([kk, vv], axis=1)


def kernel(x_prompt, x_sample, cache_kv_w128, cache_kv_w512, cache_kv_w2048, state_conv, state_pool, p_prompt, p_sample, rel_bias, norm_mix, w_in, conv_w, conv_b, conv_ln_g, conv_ln_b, w_a_out, q_norm, k_norm, w_b_out, sgu_ln_g, sgu_ln_b, sgu_w, sgu_b, w_c_out, pool_w, pool_scale, w_d_out, w_o, norm_ffn, w_gate, w_up, w_down, ple_norm, w_ple_gate, w_ple):
    bp, seq, _ = x_prompt.shape
    nb, ts, _ = x_sample.shape
    depth = w_in.shape[0]
    caches = (cache_kv_w128, cache_kv_w512, cache_kv_w2048)
    assert bp == 1 and seq % ATTN_TOKEN_BLOCK == 0
    assert ts % 8 == 0 and ts <= CHUNK
    for c, (win, _) in zip(caches, DIL_GROUPS):
        assert c.shape[2] == win, "sample caches must hold a full window"
    tm = 256

    prompt_bias = [_prompt_bias(rel_bias, g, win, dil) for g, (win, dil) in enumerate(DIL_GROUPS)]
    sample_bias = [_sample_bias(rel_bias, g, win, dil, ts) for g, (win, dil) in enumerate(DIL_GROUPS)]
    hmask = jnp.asarray(np.kron(np.eye(HEADS_PER_GROUP), np.ones((ts, HEAD_DIM))), F32)
    caches_t = [jnp.transpose(c, (0, 1, 3, 4, 5, 2)).reshape(depth, nb, 2, GROUP_W, c.shape[2])
                for c in caches]
    new_caches = None

    yp = x_prompt.reshape(seq, D_MODEL)
    ys = x_sample.reshape(nb * ts, D_MODEL)
    kvp = [[] for _ in DIL_GROUPS]
    conv_p, pool_p, conv_s, pool_s, sgu_s = [], [], [], [], []
    for i in range(depth):
        lw = _layer_weights(i, ts, norm_mix, w_in, conv_w, conv_b, conv_ln_g, conv_ln_b, w_a_out, q_norm,
                            k_norm, w_b_out, sgu_ln_g, sgu_ln_b, sgu_w, sgu_b, w_c_out, pool_w, pool_scale,
                            w_d_out, w_o, norm_ffn, w_gate, w_up, w_down, ple_norm, w_ple_gate, w_ple)

        pm, g1, q, k, v, cst, pst = _proj_prompt(yp, lw, tm)
        os_, ls_ = [], []
        for g, (win, dil) in enumerate(DIL_GROUPS):
            o, lse = _attn_prompt(q, k, v, prompt_bias[g], g, dil)
            os_.append(o)
            ls_.append(lse)
            kvp[g].append(_kv_rows(k, v, g, min(win, seq))[None])
        yp = _mix(yp, pm, g1, os_, ls_, p_prompt[i, 0], lw, tm)
        conv_p.append(cst[CONV_PAD - CONV_STATE:][None])
        pool_p.append(pst[POOL_PAD - POOL_STATE:][None])

        sconv = jnp.pad(state_conv[i], ((0, 0), (CONV_PAD - CONV_STATE, 0), (0, 0)))
        spool = jnp.pad(state_pool[i], ((0, 0), (POOL_PAD - POOL_STATE, 0), (0, 0)))
        pm, g1, q, k, v, cst, pst, vv = _proj_sample(ys, sconv, spool, lw, nb, ts, PAST_LEN)
        os_, ls_, new_caches = _attn_sample(i, q, k, v, caches_t, new_caches, sample_bias, hmask, nb, ts)
        ys = _mix(ys, pm, g1, os_, ls_, p_sample[i].reshape(nb * ts, D_PLE), lw, nb * ts)
        conv_s.append(cst[:, CONV_PAD - CONV_STATE:])
        pool_s.append(pst[:, POOL_PAD - POOL_STATE:])
        sgu_s.append(vv.reshape(nb, ts, D_SGU))

    kvs = [jnp.transpose(c.reshape(depth, nb, 2, HEADS_PER_GROUP, HEAD_DIM, c.shape[4]), (0, 1, 5, 2, 3, 4))
           for c in new_caches]
    return (yp.reshape(bp, seq, D_MODEL), ys.reshape(nb, ts, D_MODEL),
            jnp.stack(kvp[0]), jnp.stack(kvp[1]), jnp.stack(kvp[2]),
            jnp.stack(conv_p), jnp.stack(pool_p),
            kvs[0], kvs[1], kvs[2],
            jnp.stack(conv_s), jnp.stack(pool_s), jnp.stack(sgu_s))
```

```python
import functools

import numpy as np
import jax
import jax.numpy as jnp
from jax import lax
from jax.experimental import pallas as pl
from jax.experimental.pallas import tpu as pltpu

F32 = jnp.float32
BF16 = jnp.bfloat16

D_MODEL = 1024
PAST_LEN = 16384
D_CONV = 384
CONV_WIDTH = 31
CONV_STATE = CONV_WIDTH - 1
HEAD_DIM = 64
DIL_GROUPS = ((128, 1), (512, 4), (2048, 16))
HEADS_PER_GROUP = 4
N_HEADS = HEADS_PER_GROUP * len(DIL_GROUPS)
D_ATTN = N_HEADS * HEAD_DIM
GROUP_W = HEADS_PER_GROUP * HEAD_DIM
N_BUCKETS = 32
MAX_DISTANCE = 2048
D_SGU = 384
SGU_GROUPS = 4
CHUNK = 128
D_POOL = 384
POOL_WINDOWS = (2, 4, 8, 16)
POOL_STATE = max(POOL_WINDOWS) - 1
POOL_GW = D_POOL // len(POOL_WINDOWS)
D_FF = 2816
D_PLE = 256
N_BRANCH = 4
COL_A = 0
COL_Q = 2 * D_CONV
COL_K = COL_Q + D_ATTN
COL_V = COL_K + D_ATTN
COL_C = COL_V + D_ATTN
COL_D = COL_C + 2 * D_SGU
COL_G = COL_D + D_POOL
D_IN = COL_G + N_BRANCH * D_MODEL
EPS = 1e-6
LN_EPS = 1e-5
NEG_INF = -1e30

Q_BLOCK = 128
ATTN_TOKEN_BLOCK = 2048
CONV_PAD = 32
POOL_PAD = 16
KEY_PAD = 128
ROW_BLOCK = 32
V7X_VMEM_LIMIT_BYTES = 56 * 1024 * 1024


def _dot(a, b):
    return jnp.dot(a.astype(BF16), b.astype(BF16), preferred_element_type=F32)


def _dot_nt(a, b):
    return lax.dot_general(a.astype(BF16), b.astype(BF16), (((1,), (1,)), ((), ())),
                           preferred_element_type=F32)


def _rmsnorm(x, g):
    return x * lax.rsqrt(jnp.mean(x * x, axis=-1, keepdims=True) + EPS) * g


def _layernorm(x, g, b):
    mu = jnp.mean(x, axis=-1, keepdims=True)
    xc = x - mu
    return xc * lax.rsqrt(jnp.mean(xc * xc, axis=-1, keepdims=True) + LN_EPS) * g + b


def _sigmoid(x):
    return 0.5 * jnp.tanh(0.5 * x) + 0.5


def _silu(x):
    return x * _sigmoid(x)


def _head_rmsnorm(z, head_mean_ref, g):
    ms = _dot(z * z, head_mean_ref[...])
    return z * lax.rsqrt(ms + EPS) * g


def _by_channel_group(vals, n_channels):
    gw = n_channels // len(vals)
    ch = lax.broadcasted_iota(jnp.int32, (1, n_channels), 1)
    out = vals[-1]
    for g in range(len(vals) - 2, -1, -1):
        out = jnp.where(ch < (g + 1) * gw, vals[g], out)
    return out


def _proj_common(x_ref, nmix_ref, win_ref):
    h = _rmsnorm(x_ref[...], nmix_ref[...]).astype(BF16)

    def proj(lo, hi):
        return jnp.dot(h, win_ref[:, lo:hi], preferred_element_type=F32)
    return proj


def _qkv_and_gate1(proj, qn_ref, kn_ref, hm_ref, q_ref, k_ref, v_ref, g1_ref):
    q_ref[...] = _head_rmsnorm(proj(COL_Q, COL_K), hm_ref, qn_ref[...]) * (HEAD_DIM ** -0.5)
    k_ref[...] = _head_rmsnorm(proj(COL_K, COL_V), hm_ref, kn_ref[...])
    v_ref[...] = proj(COL_V, COL_C)
    g1_ref[...] = _sigmoid(proj(COL_G + D_MODEL, COL_G + 2 * D_MODEL))


def _gate(proj, branch):
    lo = COL_G + branch * D_MODEL
    return _sigmoid(proj(lo, lo + D_MODEL))


def _proj_prompt_kernel(x_ref, nmix_ref, win_ref, convw_ref, convb_ref, clng_ref, clnb_ref, wa_ref,
                        qn_ref, kn_ref, hm_ref, slng_ref, slnb_ref, swcat_ref, sbias_ref, wc_ref,
                        pwbd_ref, pscale_ref, wd_ref,
                        pm_ref, g1_ref, q_ref, k_ref, v_ref, cst_ref, pst_ref,
                        cbuf, pbuf, *, tm):
    i = pl.program_id(0)

    @pl.when(i == 0)
    def _():
        cbuf[0:CONV_PAD, :] = jnp.zeros((CONV_PAD, D_CONV), F32)
        pbuf[0:POOL_PAD, :] = jnp.zeros((POOL_PAD, D_POOL), F32)

    proj = _proj_common(x_ref, nmix_ref, win_ref)

    za = proj(COL_A, COL_Q)
    cbuf[CONV_PAD:CONV_PAD + tm, :] = za[:, :D_CONV] * _sigmoid(za[:, D_CONV:])
    zc = proj(COL_C, COL_D)
    zd = proj(COL_D, COL_G)

    gates = {}

    def emit_q():
        q_ref[...] = _head_rmsnorm(proj(COL_Q, COL_K), hm_ref, qn_ref[...]) * (HEAD_DIM ** -0.5)

    def emit_k():
        k_ref[...] = _head_rmsnorm(proj(COL_K, COL_V), hm_ref, kn_ref[...])

    def emit_v():
        v_ref[...] = proj(COL_V, COL_C)

    def emit_g1():
        g1_ref[...] = _gate(proj, 1)

    def emit_gate(branch):
        gates[branch] = _gate(proj, branch)

    fillers = [emit_q, emit_k, emit_v, emit_g1, functools.partial(emit_gate, 0),
               functools.partial(emit_gate, 2), functools.partial(emit_gate, 3)]

    cvs = []
    for blk in range(tm // ROW_BLOCK):
        base = blk * ROW_BLOCK
        acc = jnp.broadcast_to(convb_ref[...], (ROW_BLOCK, D_CONV))
        for shift in range(8):
            offs = [j + CONV_PAD - CONV_STATE for j in range(CONV_WIDTH)
                    if (j + CONV_PAD - CONV_STATE) % 8 == shift]
            rows = ROW_BLOCK if shift == 0 else ROW_BLOCK + 8
            part = None
            for off in offs:
                j = off - (CONV_PAD - CONV_STATE)
                term = convw_ref[j:j + 1, :] * cbuf[pl.ds(base + off - shift, rows), :]
                part = term if part is None else part + term
            acc = acc + part[shift:shift + ROW_BLOCK, :]
        cvs.append(_silu(_layernorm(acc, clng_ref[...], clnb_ref[...])))
        if blk < len(fillers):
            fillers[blk]()
    for filler in fillers[tm // ROW_BLOCK:]:
        filler()
    pm = gates[0] * _dot(jnp.concatenate(cvs, axis=0), wa_ref[...])
    tail = cbuf[tm:tm + CONV_PAD, :]
    cst_ref[...] = tail
    cbuf[0:CONV_PAD, :] = tail

    u = zc[:, :D_SGU]
    vv = _layernorm(zc[:, D_SGU:], slng_ref[...], slnb_ref[...])
    svs = []
    for c in range(tm // CHUNK):
        r = _dot(swcat_ref[...], vv[c * CHUNK:(c + 1) * CHUNK, :])
        sv = _by_channel_group([r[g * CHUNK:(g + 1) * CHUNK, :] for g in range(SGU_GROUPS)], D_SGU)
        svs.append(sv + sbias_ref[...])
    sv = jnp.concatenate(svs, axis=0) if len(svs) > 1 else svs[0]
    pm = pm + gates[2] * _dot(u * sv, wc_ref[...])

    pbuf[POOL_PAD:POOL_PAD + tm, :] = zd
    pooled = []
    for blk in range(tm // ROW_BLOCK):
        base = blk * ROW_BLOCK
        pos1 = i * tm + base + lax.broadcasted_iota(jnp.int32, (ROW_BLOCK, 1), 0) + 1
        run = pbuf[base:base + POOL_PAD + ROW_BLOCK, :]
        cur = run[POOL_PAD:, :]
        width = 1
        sums, cnts = [], []
        for w in POOL_WINDOWS:
            while width < w:
                run = run + pltpu.roll(run, width, 0)
                width *= 2
            sums.append(run[POOL_PAD:, :])
            cnts.append(jnp.minimum(pos1, w).astype(F32))
        pooled.append(_by_channel_group(sums, D_POOL) / _by_channel_group(cnts, D_POOL) - cur)
    pooled = jnp.concatenate(pooled, axis=0)
    mixed = _dot(pooled, pwbd_ref[...]) * pscale_ref[...]
    pm = pm + gates[3] * _dot(mixed, wd_ref[...])
    ptail = pbuf[tm:tm + POOL_PAD, :]
    pst_ref[...] = ptail
    pbuf[0:POOL_PAD, :] = ptail

    pm_ref[...] = pm


def _proj_sample_kernel(x_ref, sconv_ref, spool_ref, nmix_ref, win_ref, convw_ref, convb_ref,
                        clng_ref, clnb_ref, wa_ref, qn_ref, kn_ref, hm_ref, slng_ref, slnb_ref,
                        csgu_ref, sbias_ref, wc_ref, pwbd_ref, pscale_ref, wd_ref,
                        pm_ref, g1_ref, q_ref, k_ref, v_ref, cst_ref, pst_ref, vv_ref,
                        cbuf, pbuf, *, nb, ts, pos0):
    n = nb * ts
    proj = _proj_common(x_ref, nmix_ref, win_ref)

    za = proj(COL_A, COL_Q)
    glu = za[:, :D_CONV] * _sigmoid(za[:, D_CONV:])
    cbuf[:, 0:CONV_PAD, :] = sconv_ref[...]
    cbuf[:, CONV_PAD:CONV_PAD + ts, :] = glu.reshape(nb, ts, D_CONV)
    acc = jnp.broadcast_to(convb_ref[...].reshape(1, 1, D_CONV), (nb, ts, D_CONV))
    for j in range(CONV_WIDTH):
        w = convw_ref[j:j + 1, :].reshape(1, 1, D_CONV)
        acc = acc + w * cbuf[:, pl.ds(j + CONV_PAD - CONV_STATE, ts), :]
    cv = _silu(_layernorm(acc.reshape(n, D_CONV), clng_ref[...], clnb_ref[...]))
    pm = _gate(proj, 0) * _dot(cv, wa_ref[...])
    cst_ref[...] = cbuf[:, ts:ts + CONV_PAD, :]

    zc = proj(COL_C, COL_D)
    u = zc[:, :D_SGU]
    vv = _layernorm(zc[:, D_SGU:], slng_ref[...], slnb_ref[...])
    vv_ref[...] = vv
    vv3 = vv.reshape(nb, ts, D_SGU)
    sv = jnp.broadcast_to(sbias_ref[...].reshape(1, ts, D_SGU), (nb, ts, D_SGU))
    for j in range(ts):
        sv = sv + csgu_ref[j].reshape(1, ts, D_SGU) * vv3[:, j:j + 1, :]
    pm = pm + _gate(proj, 2) * _dot(u * sv.reshape(n, D_SGU), wc_ref[...])

    zd = proj(COL_D, COL_G)
    pbuf[:, 0:POOL_PAD, :] = spool_ref[...]
    pbuf[:, POOL_PAD:POOL_PAD + ts, :] = zd.reshape(nb, ts, D_POOL)
    pos1 = pos0 + lax.broadcasted_iota(jnp.int32, (1, ts, 1), 1) + 1
    run = jnp.zeros((nb, ts, D_POOL), F32)
    sums, cnts = [], []
    for s in range(max(POOL_WINDOWS)):
        run = run + pbuf[:, pl.ds(POOL_PAD - s, ts), :]
        if s + 1 in POOL_WINDOWS:
            sums.append(run)
            cnts.append(jnp.minimum(pos1, s + 1).astype(F32))
    ch = lax.broadcasted_iota(jnp.int32, (1, 1, D_POOL), 2)
    win_sum, cnt = sums[-1], cnts[-1]
    for g in range(len(POOL_WINDOWS) - 2, -1, -1):
        win_sum = jnp.where(ch < (g + 1) * POOL_GW, sums[g], win_sum)
        cnt = jnp.where(ch < (g + 1) * POOL_GW, cnts[g], cnt)
    pooled = (win_sum / cnt).reshape(n, D_POOL) - zd
    mixed = _dot(pooled, pwbd_ref[...]) * pscale_ref[...]
    pm = pm + _gate(proj, 3) * _dot(mixed, wd_ref[...])
    pst_ref[...] = pbuf[:, ts:ts + POOL_PAD, :]

    pm_ref[...] = pm
    _qkv_and_gate1(proj, qn_ref, kn_ref, hm_ref, q_ref, k_ref, v_ref, g1_ref)


def _split_classes(src_ref, dil, tmp, dst, dst_off):
    n = src_ref.shape[0]
    if dil == 1:
        dst[0, dst_off:dst_off + n, :] = src_ref[...]
    elif dil == 4:
        for r in range(4):
            dst[r, dst_off:dst_off + n // 4, :] = src_ref[pl.ds(r, n // 4, stride=4), :]
    else:
        assert dil == 16
        for c in range(4):
            tmp[c, 0:n // 4, :] = src_ref[pl.ds(c, n // 4, stride=4), :]
        for c in range(4):
            for d in range(4):
                dst[c + 4 * d, dst_off:dst_off + n // 16, :] = tmp[c, pl.ds(d, n // 16, stride=4), :]


def _merge_classes(dst_ref, dil, tmp, src):
    n = dst_ref.shape[0]
    if dil == 1:
        dst_ref[...] = src[0]
    elif dil == 4:
        for r in range(4):
            dst_ref[pl.ds(r, n // 4, stride=4), :] = src[r]
    else:
        assert dil == 16
        for c in range(4):
            for d in range(4):
                tmp[c, pl.ds(d, n // 16, stride=4), :] = src[c + 4 * d]
        for c in range(4):
            dst_ref[pl.ds(c, n // 4, stride=4), :] = tmp[c, 0:n // 4, :]


def _attn_prompt_kernel(q_ref, hk_ref, kc_ref, hv_ref, vc_ref, bias_ref, o_ref, l_ref,
                        qs, ks, vs, s_scr, p_scr, m_scr, os_, ls_, b0_scr, tmp, *, dil):
    tb = q_ref.shape[0]
    n_sub = tb // (dil * Q_BLOCK)
    units = [(r, j) for r in range(dil) for j in range(n_sub)]
    first = pl.program_id(0) == 0

    _split_classes(q_ref, dil, tmp, qs, 0)
    _split_classes(hk_ref, dil, tmp, ks, 0)
    _split_classes(kc_ref, dil, tmp, ks, Q_BLOCK)
    _split_classes(hv_ref, dil, tmp, vs, 0)
    _split_classes(vc_ref, dil, tmp, vs, Q_BLOCK)

    col = lax.broadcasted_iota(jnp.int32, (2 * Q_BLOCK, 2 * Q_BLOCK), 1)
    b0_scr[...] = jnp.where(jnp.logical_and(col < Q_BLOCK, first), NEG_INF, bias_ref[0])

    lane = lax.broadcasted_iota(jnp.int32, (1, 128), 1)
    head0 = lane < HEAD_DIM
    for u, (r, j) in enumerate(units):
        q = qs[r, j * Q_BLOCK:(j + 1) * Q_BLOCK, :]
        q2 = jnp.concatenate([jnp.where(head0, q, 0.0), jnp.where(head0, 0.0, q)], axis=0)
        bias = b0_scr[...] if j == 0 else bias_ref[0]
        s_scr[u] = _dot_nt(q2, ks[r, j * Q_BLOCK:(j + 2) * Q_BLOCK, :]) + bias

    for u in range(len(units)):
        s = s_scr[u]
        m = jnp.max(s, axis=-1, keepdims=True)
        p_scr[u] = jnp.exp(s - m).astype(BF16)
        m_scr[u] = jnp.broadcast_to(m, (2 * Q_BLOCK, 128))

    ones = jnp.ones((2 * Q_BLOCK, 128), BF16)
    for u, (r, j) in enumerate(units):
        v = vs[r, j * Q_BLOCK:(j + 2) * Q_BLOCK, :].astype(BF16)
        o2 = jnp.dot(p_scr[u], jnp.concatenate([v, ones], axis=1), preferred_element_type=F32)
        num = jnp.where(head0, o2[:Q_BLOCK, :128], o2[Q_BLOCK:, :128])
        den = jnp.where(head0, o2[:Q_BLOCK, 128:], o2[Q_BLOCK:, 128:])
        mx = jnp.where(head0, m_scr[u, :Q_BLOCK, :], m_scr[u, Q_BLOCK:, :])
        os_[r, j * Q_BLOCK:(j + 1) * Q_BLOCK, :] = num / den
        ls_[r, j * Q_BLOCK:(j + 1) * Q_BLOCK, :] = mx + jnp.log(den)

    _merge_classes(o_ref, dil, tmp, os_)
    _merge_classes(l_ref, dil, tmp, ls_)


def _attn_sample_kernel(*refs, ts, aliased):
    n_g = len(DIL_GROUPS)
    q_ref, k_ref, v_ref = refs[0:3]
    caches = refs[3:3 + n_g]
    biases = refs[3 + n_g:3 + 2 * n_g]
    hmask_ref = refs[3 + 2 * n_g]
    pos = 4 + 2 * n_g + (n_g if aliased else 0)
    outs = refs[pos:pos + n_g]
    lses = refs[pos + n_g:pos + 2 * n_g]
    news = refs[pos + 2 * n_g:pos + 3 * n_g]
    exts = refs[pos + 3 * n_g:pos + 4 * n_g]

    for g in range(n_g):
        ext = exts[g]
        length = ext.shape[2] - KEY_PAD
        gs = slice(g * GROUP_W, (g + 1) * GROUP_W)
        zeros = jnp.zeros((KEY_PAD - ts, GROUP_W), F32)
        for kv, new_ref in enumerate((k_ref, v_ref)):
            ext[kv, :, 0:length] = caches[g][0, 0, kv]
            ext[kv, :, length:length + KEY_PAD] = jnp.concatenate([new_ref[:, gs], zeros], axis=0).T
            news[g][0, 0, kv] = ext[kv][:, ts:ts + length]
        for later in range(1, news[g].shape[0]):
            news[g][later, 0] = jnp.zeros(news[g].shape[2:], F32)

        qm = jnp.concatenate([q_ref[:, gs]] * HEADS_PER_GROUP, axis=0) * hmask_ref[...]
        s = _dot(qm, ext[0]) + biases[g][...]
        m = jnp.max(s, axis=-1, keepdims=True)
        p = jnp.exp(s - m)
        den = jnp.sum(p, axis=-1, keepdims=True)
        o = _dot_nt(p, ext[1]) / den
        lse = m + jnp.log(den)
        for h in range(HEADS_PER_GROUP):
            sl = slice(h * HEAD_DIM, (h + 1) * HEAD_DIM)
            rows = slice(h * ts, (h + 1) * ts)
            outs[g][:, sl] = o[rows, sl]
            lses[g][:, sl] = jnp.broadcast_to(lse[rows, :], (ts, HEAD_DIM))


def _mix_kernel(x_ref, pm_ref, g1_ref, o1_ref, o2_ref, o3_ref, l1_ref, l2_ref, l3_ref, pe_ref,
                wb_ref, wo_ref, nffn_ref, wg_ref, wu_ref, wdn_ref, pnorm_ref, wpg_ref, wp_ref, y_ref):
    l1, l2, l3 = l1_ref[...], l2_ref[...], l3_ref[...]
    mx = jnp.maximum(jnp.maximum(l1, l2), l3)
    e1, e2, e3 = jnp.exp(l1 - mx), jnp.exp(l2 - mx), jnp.exp(l3 - mx)
    o = (e1 * o1_ref[...] + e2 * o2_ref[...] + e3 * o3_ref[...]) / (e1 + e2 + e3)
    merged = pm_ref[...] + g1_ref[...] * _dot(o, wb_ref[...])
    x = x_ref[...] + _dot(merged, wo_ref[...])
    h2 = _rmsnorm(x, nffn_ref[...]).astype(BF16)
    ff = _silu(_dot(h2, wg_ref[...])) * _dot(h2, wu_ref[...])
    x = x + _dot(ff, wdn_ref[...])
    h3 = _rmsnorm(x, pnorm_ref[...])
    y_ref[...] = x + _dot(pe_ref[...], wp_ref[...]) * _sigmoid(_dot(h3, wpg_ref[...]))


def _resident(stacked, layer):
    layer = layer if stacked.shape[0] > 1 else 0
    rest = stacked.shape[1:]
    return pl.BlockSpec((None,) + rest, lambda *_: (layer,) + (0,) * len(rest),
                        pipeline_mode=pl.Buffered(1))


def _rows(tm, width):
    return pl.BlockSpec((tm, width), lambda i: (i, 0))


def _params(n_axes, flags=None):
    return pltpu.CompilerParams(dimension_semantics=("arbitrary",) * n_axes,
                                vmem_limit_bytes=V7X_VMEM_LIMIT_BYTES, flags=flags)


def _proj_weight_args(lw):
    return (lw["nmix"], lw["w_in"], lw["conv_w"], lw["conv_b"], lw["cln_g"], lw["cln_b"], lw["w_a"],
            lw["qn"], lw["kn"], lw["head_mean"], lw["sln_g"], lw["sln_b"])


def _proj_prompt(layer, x, lw, tm):
    t = x.shape[0]
    mid = (lw["sgu_wcat"], lw["sgu_bias"], lw["w_c"], lw["pool_wbd"], lw["pool_scale"], lw["w_d"])
    weights = _proj_weight_args(lw) + mid
    f = lambda w: jax.ShapeDtypeStruct((t, w), F32)
    return pl.pallas_call(
        functools.partial(_proj_prompt_kernel, tm=tm),
        grid=(t // tm,),
        in_specs=[_rows(tm, D_MODEL)] + [_resident(w, layer) for w in weights],
        out_specs=[_rows(tm, D_MODEL), _rows(tm, D_MODEL), _rows(tm, D_ATTN), _rows(tm, D_ATTN),
                   _rows(tm, D_ATTN),
                   pl.BlockSpec((CONV_PAD, D_CONV), lambda i: (0, 0)),
                   pl.BlockSpec((POOL_PAD, D_POOL), lambda i: (0, 0))],
        out_shape=[f(D_MODEL), f(D_MODEL), f(D_ATTN), f(D_ATTN), f(D_ATTN),
                   jax.ShapeDtypeStruct((CONV_PAD, D_CONV), F32),
                   jax.ShapeDtypeStruct((POOL_PAD, D_POOL), F32)],
        scratch_shapes=[pltpu.VMEM((tm + CONV_PAD, D_CONV), F32),
                        pltpu.VMEM((tm + POOL_PAD, D_POOL), F32)],
        compiler_params=_params(1),
        name="proj_prompt",
    )(x, *weights)


def _proj_sample(layer, x, sconv, spool, lw, nb, ts, pos0):
    n = nb * ts
    mid = (lw["sgu_c8"], lw["sgu_bias8"], lw["w_c"], lw["pool_wbd"], lw["pool_scale"], lw["w_d"])
    weights = _proj_weight_args(lw) + mid
    f = lambda w: jax.ShapeDtypeStruct((n, w), F32)
    full = lambda a: pl.BlockSpec(a.shape, lambda i, nd=a.ndim: (0,) * nd)
    return pl.pallas_call(
        functools.partial(_proj_sample_kernel, nb=nb, ts=ts, pos0=pos0),
        grid=(1,),
        in_specs=[full(a) for a in (x, sconv, spool)] + [_resident(w, layer) for w in weights],
        out_specs=[pl.BlockSpec((n, D_MODEL), lambda i: (0, 0)), pl.BlockSpec((n, D_MODEL), lambda i: (0, 0)),
                   pl.BlockSpec((n, D_ATTN), lambda i: (0, 0)), pl.BlockSpec((n, D_ATTN), lambda i: (0, 0)),
                   pl.BlockSpec((n, D_ATTN), lambda i: (0, 0)),
                   pl.BlockSpec((nb, CONV_PAD, D_CONV), lambda i: (0, 0, 0)),
                   pl.BlockSpec((nb, POOL_PAD, D_POOL), lambda i: (0, 0, 0)),
                   pl.BlockSpec((n, D_SGU), lambda i: (0, 0))],
        out_shape=[f(D_MODEL), f(D_MODEL), f(D_ATTN), f(D_ATTN), f(D_ATTN),
                   jax.ShapeDtypeStruct((nb, CONV_PAD, D_CONV), F32),
                   jax.ShapeDtypeStruct((nb, POOL_PAD, D_POOL), F32),
                   f(D_SGU)],
        scratch_shapes=[pltpu.VMEM((nb, CONV_PAD + ts, D_CONV), F32),
                        pltpu.VMEM((nb, POOL_PAD + ts, D_POOL), F32)],
        compiler_params=_params(1),
        name="proj_sample",
    )(x, sconv, spool, *weights)


def _attn_prompt(q, k, v, bias, g, dil):
    t = q.shape[0]
    tb = ATTN_TOKEN_BLOCK
    hb = dil * Q_BLOCK
    lane_blocks = GROUP_W // 128
    cur = pl.BlockSpec((tb, 128), lambda b, hp: (b, lane_blocks * g + hp))
    halo = pl.BlockSpec((hb, 128), lambda b, hp: (jnp.maximum(b * (tb // hb) - 1, 0), lane_blocks * g + hp))
    out = pl.BlockSpec((tb, 128), lambda b, hp: (b, hp))
    n_sub = tb // hb
    n_units = dil * n_sub
    rows2 = 2 * Q_BLOCK
    o, lse = pl.pallas_call(
        functools.partial(_attn_prompt_kernel, dil=dil),
        grid=(t // tb, lane_blocks),
        in_specs=[cur, halo, cur, halo, cur,
                  pl.BlockSpec((1, rows2, 2 * Q_BLOCK), lambda b, hp: (hp, 0, 0))],
        out_specs=[out, out],
        out_shape=[jax.ShapeDtypeStruct((t, GROUP_W), F32)] * 2,
        scratch_shapes=[pltpu.VMEM((dil, n_sub * Q_BLOCK, 128), F32),
                        pltpu.VMEM((dil, (n_sub + 1) * Q_BLOCK, 128), F32),
                        pltpu.VMEM((dil, (n_sub + 1) * Q_BLOCK, 128), F32),
                        pltpu.VMEM((n_units, rows2, 2 * Q_BLOCK), F32),
                        pltpu.VMEM((n_units, rows2, 2 * Q_BLOCK), BF16),
                        pltpu.VMEM((n_units, rows2, 128), F32),
                        pltpu.VMEM((dil, n_sub * Q_BLOCK, 128), F32),
                        pltpu.VMEM((dil, n_sub * Q_BLOCK, 128), F32),
                        pltpu.VMEM((rows2, 2 * Q_BLOCK), F32),
                        pltpu.VMEM((4, tb // 4, 128), F32)],
        compiler_params=_params(2),
        name=f"attn_prompt_d{dil}",
    )(q, k, k, v, v, bias.reshape(lane_blocks, rows2, 2 * Q_BLOCK))
    return o, lse


def _attn_sample(layer, q, k, v, caches_t, new_prev, biases, hmask, nb, ts):
    n = nb * ts
    aliased = new_prev is not None
    tok = lambda w: pl.BlockSpec((ts, w), lambda b: (b, 0))
    cache_spec = lambda c: pl.BlockSpec((1, 1) + c.shape[2:], lambda b: (layer, b, 0, 0, 0))
    if aliased:
        new_spec = cache_spec
    else:
        assert layer == 0
        new_spec = lambda c: pl.BlockSpec((c.shape[0], 1) + c.shape[2:], lambda b: (0, b, 0, 0, 0))
    const = lambda a: pl.BlockSpec(a.shape, lambda b: (0, 0))
    n_in = 3 + 2 * len(caches_t) + 1
    res = pl.pallas_call(
        functools.partial(_attn_sample_kernel, ts=ts, aliased=aliased),
        grid=(nb,),
        in_specs=[tok(D_ATTN)] * 3 + [cache_spec(c) for c in caches_t] + [const(b) for b in biases]
                 + [const(hmask)]
                 + ([pl.BlockSpec(memory_space=pl.ANY)] * len(caches_t) if aliased else []),
        out_specs=[tok(GROUP_W)] * 6 + [new_spec(c) for c in caches_t],
        out_shape=[jax.ShapeDtypeStruct((n, GROUP_W), F32)] * 6
                  + [jax.ShapeDtypeStruct(c.shape, F32) for c in caches_t],
        scratch_shapes=[pltpu.VMEM((2, GROUP_W, c.shape[4] + KEY_PAD), F32) for c in caches_t],
        input_output_aliases={n_in + i: 6 + i for i in range(len(caches_t))} if aliased else {},
        compiler_params=_params(1),
        name="attn_sample",
    )(q, k, v, *caches_t, *biases, hmask, *(new_prev if aliased else ()))
    return res[0:3], res[3:6], res[6:9]


def _mix(layer, x, pm, g1, os_, ls_, pe, lw, tm):
    t = x.shape[0]
    weights = (lw["w_b"], lw["w_o"], lw["nffn"], lw["w_gate"], lw["w_up"], lw["w_down"],
               lw["pnorm"], lw["w_pg"], lw["w_p"])
    return pl.pallas_call(
        _mix_kernel,
        grid=(t // tm,),
        in_specs=[_rows(tm, D_MODEL)] * 3 + [_rows(tm, GROUP_W)] * 6
                 + [pl.BlockSpec((None, tm, D_PLE), lambda i: (layer, i, 0))]
                 + [_resident(w, layer) for w in weights],
        out_specs=_rows(tm, D_MODEL),
        out_shape=jax.ShapeDtypeStruct((t, D_MODEL), F32),
        compiler_params=_params(1),
        name="mix",
    )(x, pm, g1, *os_, *ls_, pe, *weights)


def _t5_bucket(dist):
    max_exact = N_BUCKETS // 2
    d = np.asarray(dist)
    large = max_exact + (np.log(np.maximum(d, 1) / max_exact)
                         / np.log(MAX_DISTANCE / max_exact)
                         * (N_BUCKETS - max_exact)).astype(np.int64)
    large = np.minimum(large, N_BUCKETS - 1)
    return np.where(d < max_exact, d, large).astype(np.int32)


def _bias_by_key(rel_bias, g, win, dil):
    buckets = _t5_bucket(dil * np.arange(win // dil + 1))
    cols = rel_bias[:, g * HEADS_PER_GROUP:(g + 1) * HEADS_PER_GROUP].T
    edges = [0] + [j for j in range(1, len(buckets)) if buckets[j] != buckets[j - 1]] + [len(buckets)]
    runs = [jnp.broadcast_to(cols[:, buckets[a]:buckets[a] + 1], (HEADS_PER_GROUP, b - a))
            for a, b in zip(edges[:-1], edges[1:])]
    return jnp.concatenate(runs, axis=1).astype(F32)


def _prompt_bias(rel_bias, g, win, dil):
    nk = win // dil
    assert nk == Q_BLOCK
    neg = jnp.full((HEADS_PER_GROUP, Q_BLOCK - 1), NEG_INF, F32)
    diag = jnp.concatenate([neg, _bias_by_key(rel_bias, g, win, dil)[:, ::-1], neg, neg[:, :1]], axis=1)
    period = 3 * Q_BLOCK
    skew = jnp.tile(diag, (1, Q_BLOCK))[:, :Q_BLOCK * (period - 1)]
    return skew.reshape(HEADS_PER_GROUP, Q_BLOCK, period - 1)[:, :, Q_BLOCK - 1:]


def _sample_bias(rel_bias, g, win, dil, ts):
    by_key = _bias_by_key(rel_bias, g, win, dil)[:, ::-1]
    holes = jnp.full((HEADS_PER_GROUP, win // dil + 1, dil - 1), NEG_INF, F32)
    by_dist = jnp.concatenate([by_key[:, :, None], holes], axis=2).reshape(HEADS_PER_GROUP, -1)[:, :win + 1]
    rows = [jnp.pad(by_dist, ((0, 0), (t, KEY_PAD - 1 - t)), constant_values=NEG_INF) for t in range(ts)]
    return jnp.stack(rows, axis=1).reshape(HEADS_PER_GROUP * ts, win + KEY_PAD)


def _stacked_weights(ts, norm_mix, w_in, conv_w, conv_b, conv_ln_g, conv_ln_b, w_a_out, q_norm, k_norm,
                     w_b_out, sgu_ln_g, sgu_ln_b, sgu_w, sgu_b, w_c_out, pool_w, pool_scale, w_d_out,
                     w_o, norm_ffn, w_gate, w_up, w_down, ple_norm, w_ple_gate, w_ple):
    depth = w_in.shape[0]
    row = lambda a: a.reshape(depth, 1, -1).astype(F32)
    bf = lambda a: a.astype(BF16)
    tril = np.tril(np.ones((CHUNK, CHUNK), dtype=bool))
    ws = jnp.where(tril[None, None], sgu_w, 0.0)
    gw = D_SGU // SGU_GROUPS
    pool_bd = jnp.zeros((depth, D_POOL, D_POOL), F32)
    for g in range(len(POOL_WINDOWS)):
        pool_bd = lax.dynamic_update_slice(pool_bd, pool_w[:, g], (0, g * POOL_GW, g * POOL_GW))
    head_mean = np.kron(np.eye(N_HEADS), np.full((HEAD_DIM, HEAD_DIM), 1.0 / HEAD_DIM))
    sgu_bias = jnp.repeat(jnp.transpose(sgu_b, (0, 2, 1)), gw, axis=2)
    return {
        "nmix": row(norm_mix), "w_in": bf(w_in), "conv_w": conv_w, "conv_b": row(conv_b),
        "cln_g": row(conv_ln_g), "cln_b": row(conv_ln_b), "w_a": bf(w_a_out),
        "qn": row(jnp.tile(q_norm, (1, N_HEADS))), "kn": row(jnp.tile(k_norm, (1, N_HEADS))),
        "head_mean": jnp.asarray(head_mean, BF16)[None],
        "sln_g": row(sgu_ln_g), "sln_b": row(sgu_ln_b),
        "sgu_wcat": ws.reshape(depth, SGU_GROUPS * CHUNK, CHUNK).astype(BF16),
        "sgu_c8": jnp.repeat(jnp.transpose(ws[:, :, :ts, :ts], (0, 3, 2, 1)), gw, axis=3),
        "sgu_bias": sgu_bias, "sgu_bias8": sgu_bias[:, :ts],
        "w_c": bf(w_c_out), "pool_wbd": pool_bd.astype(BF16), "pool_scale": row(pool_scale),
        "w_d": bf(w_d_out), "w_b": bf(w_b_out), "w_o": bf(w_o), "nffn": row(norm_ffn),
        "w_gate": bf(w_gate), "w_up": bf(w_up), "w_down": bf(w_down), "pnorm": row(ple_norm),
        "w_pg": bf(w_ple_gate), "w_p": bf(w_ple),
    }


def _kv_rows(k, v, g, keep):
    gs = slice(g * GROUP_W, (g + 1) * GROUP_W)
    t = k.shape[0]
    kk = k[t - keep:, gs].reshape(keep, HEADS_PER_GROUP, HEAD_DIM)
    vv = v[t - keep:, gs].reshape(keep, HEADS_PER_GROUP, HEAD_DIM)
    return jnp.stack([kk, vv], axis=1)


def kernel(x_prompt, x_sample, cache_kv_w128, cache_kv_w512, cache_kv_w2048, state_conv, state_pool, p_prompt, p_sample, rel_bias, norm_mix, w_in, conv_w, conv_b, conv_ln_g, conv_ln_b, w_a_out, q_norm, k_norm, w_b_out, sgu_ln_g, sgu_ln_b, sgu_w, sgu_b, w_c_out, pool_w, pool_scale, w_d_out, w_o, norm_ffn, w_gate, w_up, w_down, ple_norm, w_ple_gate, w_ple):
    bp, seq, _ = x_prompt.shape
    nb, ts, _ = x_sample.shape
    depth = w_in.shape[0]
    caches = (cache_kv_w128, cache_kv_w512, cache_kv_w2048)
    assert bp == 1 and seq % ATTN_TOKEN_BLOCK == 0
    assert ts % 8 == 0 and ts <= CHUNK
    for c, (win, _) in zip(caches, DIL_GROUPS):
        assert c.shape[2] == win, "sample caches must hold a full window"
    tm = 512

    prompt_bias = [_prompt_bias(rel_bias, g, win, dil) for g, (win, dil) in enumerate(DIL_GROUPS)]
    sample_bias = [_sample_bias(rel_bias, g, win, dil, ts) for g, (win, dil) in enumerate(DIL_GROUPS)]
    hmask = jnp.asarray(np.kron(np.eye(HEADS_PER_GROUP), np.ones((ts, HEAD_DIM))), F32)
    caches_t = [jnp.transpose(c, (0, 1, 3, 4, 5, 2)).reshape(depth, nb, 2, GROUP_W, c.shape[2])
                for c in caches]
    new_caches = None

    yp = x_prompt.reshape(seq, D_MODEL)
    ys = x_sample.reshape(nb * ts, D_MODEL)
    kvp = [[] for _ in DIL_GROUPS]
    conv_p, pool_p, conv_s, pool_s, sgu_s = [], [], [], [], []
    lw = _stacked_weights(ts, norm_mix, w_in, conv_w, conv_b, conv_ln_g, conv_ln_b, w_a_out, q_norm,
                          k_norm, w_b_out, sgu_ln_g, sgu_ln_b, sgu_w, sgu_b, w_c_out, pool_w, pool_scale,
                          w_d_out, w_o, norm_ffn, w_gate, w_up, w_down, ple_norm, w_ple_gate, w_ple)
    pe_prompt = p_prompt.reshape(depth, seq, D_PLE)
    pe_sample = p_sample.reshape(depth, nb * ts, D_PLE)
    for i in range(depth):
        pm, g1, q, k, v, cst, pst = _proj_prompt(i, yp, lw, tm)
        os_, ls_ = [], []
        for g, (win, dil) in enumerate(DIL_GROUPS):
            o, lse = _attn_prompt(q, k, v, prompt_bias[g], g, dil)
            os_.append(o)
            ls_.append(lse)
            kvp[g].append(_kv_rows(k, v, g, min(win, seq))[None])
        yp = _mix(i, yp, pm, g1, os_, ls_, pe_prompt, lw, tm)
        conv_p.append(cst[CONV_PAD - CONV_STATE:][None])
        pool_p.append(pst[POOL_PAD - POOL_STATE:][None])

        sconv = jnp.pad(state_conv[i], ((0, 0), (CONV_PAD - CONV_STATE, 0), (0, 0)))
        spool = jnp.pad(state_pool[i], ((0, 0), (POOL_PAD - POOL_STATE, 0), (0, 0)))
        pm, g1, q, k, v, cst, pst, vv = _proj_sample(i, ys, sconv, spool, lw, nb, ts, PAST_LEN)
        os_, ls_, new_caches = _attn_sample(i, q, k, v, caches_t, new_caches, sample_bias, hmask, nb, ts)
        ys = _mix(i, ys, pm, g1, os_, ls_, pe_sample, lw, nb * ts)
        conv_s.append(cst[:, CONV_PAD - CONV_STATE:])
        pool_s.append(pst[:, POOL_PAD - POOL_STATE:])
        sgu_s.append(vv.reshape(nb, ts, D_SGU))

    kvs = [jnp.transpose(c.reshape(depth, nb, 2, HEADS_PER_GROUP, HEAD_DIM, c.shape[4]), (0, 1, 5, 2, 3, 4))
           for c in new_caches]
    return (yp.reshape(bp, seq, D_MODEL), ys.reshape(nb, ts, D_MODEL),
            jnp.stack(kvp[0]), jnp.stack(kvp[1]), jnp.stack(kvp[2]),
            jnp.stack(conv_p), jnp.stack(pool_p),
            kvs[0], kvs[1], kvs[2],
            jnp.stack(conv_s), jnp.stack(pool_s), jnp.stack(sgu_s))
```
